```python
import numpy as np
import jax
import jax.numpy as jnp
from jax import lax

D_MODEL = 2048
BATCH = 8
SEQ = 4096
DEPTH = 4

CTX_LEN = 256
GRID_W = 64
H_RET = 8
RET_DK = 128
RET_DV = 128
RET_QK_W = H_RET * RET_DK
RET_V_W = H_RET * RET_DV
RET_CHUNK = 128
ROPE_THETA = 10000.0
CONV_W = 1024
CONV_K = 31
IN_W = 2 * RET_QK_W + 2 * RET_V_W + 2 * CONV_W + 2 * D_MODEL
N_EXPERTS = 32
TOP_K = 4
D_EXPERT = 512
SWIGLU_LIMIT = 7.0
SWIGLU_ALPHA = 1.702
N_MOD = 6
RMS_EPS = 1e-6
LN_EPS = 1e-5

kernel_name = 'hybrid_retention_conformer_moe_dit'


def _rmsnorm(x, g):
    xf = x.astype(jnp.float32)
    y = xf * lax.rsqrt(jnp.mean(xf * xf, axis=-1, keepdims=True) + RMS_EPS)
    return (y * g.astype(jnp.float32)).astype(x.dtype)


def _layernorm(x, g, b):
    xf = x.astype(jnp.float32)
    mu = jnp.mean(xf, axis=-1, keepdims=True)
    var = jnp.mean(jnp.square(xf - mu), axis=-1, keepdims=True)
    y = (xf - mu) * lax.rsqrt(var + LN_EPS)
    return (y * g.astype(jnp.float32) + b.astype(jnp.float32)).astype(x.dtype)


def _head_groupnorm(o):
    of = o.astype(jnp.float32)
    mu = jnp.mean(of, axis=-1, keepdims=True)
    var = jnp.mean(jnp.square(of - mu), axis=-1, keepdims=True)
    return ((of - mu) * lax.rsqrt(var + LN_EPS)).astype(o.dtype)


def _rope_2d(x, rows, cols):
    quarter = x.shape[-1] // 4
    half = 2 * quarter
    inv = ROPE_THETA ** (-jnp.arange(quarter, dtype=jnp.float32) / quarter)

    def rot(xp, pos):
        ang = pos[:, None] * inv[None, :]
        cos = jnp.cos(ang)[:, None, :].astype(xp.dtype)
        sin = jnp.sin(ang)[:, None, :].astype(xp.dtype)
        x1, x2 = xp[..., :quarter], xp[..., quarter:]
        return jnp.concatenate([x1 * cos - x2 * sin, x1 * sin + x2 * cos], axis=-1)

    return jnp.concatenate([rot(x[..., :half], rows), rot(x[..., half:], cols)], axis=-1)


def _chunk_retention(q, k, v, log_gamma, state0, strict):
    b, n_tok, h, dk = q.shape
    dv = v.shape[-1]
    nc = n_tok // RET_CHUNK
    q = q.reshape(b, nc, RET_CHUNK, h, dk)
    k = k.reshape(b, nc, RET_CHUNK, h, dk)
    v = v.reshape(b, nc, RET_CHUNK, h, dv)
    lg = log_gamma.astype(jnp.float32)
    idx = jnp.arange(RET_CHUNK, dtype=jnp.float32)
    diff = idx[:, None] - idx[None, :]
    mask = (diff > 0) if strict else (diff >= 0)
    dmask = jnp.where(mask[None], jnp.exp(lg[:, None, None] * jnp.where(mask, diff, 0.0)[None]), 0.0).astype(q.dtype)
    s = jnp.einsum('bnihd,bnjhd->bnhij', q, k) * dmask
    o = jnp.einsum('bnhij,bnjhv->bnihv', s, v)
    zeta = jnp.exp(lg[None, :] * (RET_CHUNK - 1.0 - idx)[:, None]).astype(q.dtype)
    kv = jnp.einsum('bnjhd,bnjhv->nbhdv', k * zeta[:, :, None], v)
    decay_chunk = jnp.exp(lg * RET_CHUNK).astype(q.dtype)[None, :, None, None]

    def step(state, kv_n):
        return decay_chunk * state + kv_n, state

    state_f, prev = lax.scan(step, state0.astype(kv.dtype), kv)
    xi = jnp.exp(lg[None, :] * (idx + 1.0)[:, None]).astype(q.dtype)
    o = o + jnp.einsum('bnihd,nbhdv->bnihv', q * xi[:, :, None], prev)
    return o.reshape(b, n_tok, h, dv), state_f


def _bidir_retention(q, k, v, lg_f, lg_b, n_ctx):
    b, _, h, dk = q.shape
    dv = v.shape[-1]
    zero = jnp.zeros((b, h, dk, dv), v.dtype)
    rev = lambda a: jnp.flip(a, axis=1)
    qc, ql = q[:, :n_ctx], q[:, n_ctx:]
    kc, kl = k[:, :n_ctx], k[:, n_ctx:]
    vc, vl = v[:, :n_ctx], v[:, n_ctx:]
    oc_f, sc_f = _chunk_retention(qc, kc, vc, lg_f, zero, False)
    ol_f, _ = _chunk_retention(ql, kl, vl, lg_f, sc_f, False)
    oc_b, sc_b = _chunk_retention(rev(qc), rev(kc), rev(vc), lg_b, zero, True)
    ol_b, _ = _chunk_retention(rev(ql), rev(kl), rev(vl), lg_b, sc_b, True)
    return jnp.concatenate([oc_f + rev(oc_b), ol_f + rev(ol_b)], axis=1)


def _dwconv(u, w, b):
    y = lax.conv_general_dilated(u, w[:, None, :].astype(u.dtype), window_strides=(1,),
                                 padding=[(CONV_K // 2, CONV_K // 2)],
                                 dimension_numbers=('NWC', 'WIO', 'NWC'),
                                 feature_group_count=u.shape[-1])
    return y + b.astype(u.dtype)


def _mixer(h, n_ctx, rows, cols, w_in, dec_f, dec_b, w_ret_o, conv_w, conv_b, ln_g, ln_b, w_conv_o, w_out):
    bsz, t, _ = h.shape
    p = h @ w_in
    splits = np.cumsum([RET_QK_W, RET_QK_W, RET_V_W, RET_V_W, CONV_W, CONV_W, D_MODEL])
    q, k, v, g, ga, gb, s_ret, s_conv = jnp.split(p, [int(s) for s in splits], axis=-1)
    q = q.reshape(bsz, t, H_RET, RET_DK)
    k = k.reshape(bsz, t, H_RET, RET_DK) * (RET_DK ** -0.5)
    v = v.reshape(bsz, t, H_RET, RET_DV)
    q = jnp.concatenate([q[:, :n_ctx], _rope_2d(q[:, n_ctx:], rows, cols)], axis=1)
    k = jnp.concatenate([k[:, :n_ctx], _rope_2d(k[:, n_ctx:], rows, cols)], axis=1)
    o = _bidir_retention(q, k, v, jax.nn.log_sigmoid(dec_f), jax.nn.log_sigmoid(dec_b), n_ctx)
    o = _head_groupnorm(o).reshape(bsz, t, RET_V_W) * jax.nn.silu(g)
    y_ret = o @ w_ret_o
    u = ga * jax.nn.sigmoid(gb)
    u = jnp.concatenate([_dwconv(u[:, :n_ctx], conv_w, conv_b), _dwconv(u[:, n_ctx:], conv_w, conv_b)], axis=1)
    u = jax.nn.silu(_layernorm(u, ln_g, ln_b))
    y_conv = u @ w_conv_o
    merged = jax.nn.sigmoid(s_ret) * y_ret + jax.nn.sigmoid(s_conv) * y_conv
    return merged @ w_out


def _moe(h, router_w, router_b, wg, bg, wu, bu, wd, bd):
    bsz, t, d = h.shape
    xt = h.reshape(bsz * t, d)
    logits = (xt @ router_w + router_b).astype(jnp.float32)
    top_v, top_i = lax.top_k(logits, TOP_K)
    probs = jax.nn.softmax(top_v, axis=-1)
    gates = jnp.sum(jax.nn.one_hot(top_i, N_EXPERTS, dtype=jnp.float32) * probs[..., None], axis=1)
    out = jnp.zeros_like(xt)
    for e in range(N_EXPERTS):
        a = jnp.minimum(xt @ wg[e] + bg[e], SWIGLU_LIMIT)
        b = jnp.clip(xt @ wu[e] + bu[e], -SWIGLU_LIMIT, SWIGLU_LIMIT)
        y = ((b + 1.0) * a * jax.nn.sigmoid(SWIGLU_ALPHA * a)) @ wd[e] + bd[e]
        out = out + gates[:, e:e + 1].astype(y.dtype) * y
    return out.reshape(bsz, t, d)


def _modulate(hn, n_ctx, shift_c, scale_c, shift_l, scale_l):
    hc = hn[:, :n_ctx] * (1.0 + scale_c) + shift_c
    hl = hn[:, n_ctx:] * (1.0 + scale_l[:, None]) + shift_l[:, None]
    return jnp.concatenate([hc, hl], axis=1)


def _gated_residual(z, y, n_ctx, gate_c, gate_l):
    return z + jnp.concatenate([gate_c * y[:, :n_ctx], gate_l[:, None] * y[:, n_ctx:]], axis=1)


def setup_inputs(seed: int = 0) -> dict:
    key = jax.random.key(seed)
    ks = jax.random.split(key, 32)

    def nrm(k, shape, scale):
        return jax.random.normal(k, shape, jnp.float32) * scale

    base_logit = jnp.log(2.0 ** (5.0 + jnp.arange(H_RET, dtype=jnp.float32)) - 1.0)
    return {
        'x': nrm(ks[0], (BATCH, SEQ, D_MODEL), 1.0),
        'c': nrm(ks[1], (BATCH, D_MODEL), 1.0),
        'ctx': nrm(ks[2], (BATCH, CTX_LEN, D_MODEL), 1.0),
        'c_ctx': nrm(ks[3], (D_MODEL,), 1.0),
        'ada_w': nrm(ks[4], (DEPTH, D_MODEL, N_MOD * D_MODEL), 0.5 * D_MODEL ** -0.5),
        'ada_b': nrm(ks[5], (DEPTH, N_MOD * D_MODEL), 0.01),
        'norm1_g': 1.0 + nrm(ks[6], (DEPTH, D_MODEL), 0.01),
        'w_in': nrm(ks[7], (DEPTH, D_MODEL, IN_W), D_MODEL ** -0.5),
        'ret_decay_fwd': base_logit[None, :] + nrm(ks[8], (DEPTH, H_RET), 0.1),
        'ret_decay_bwd': base_logit[None, :] + nrm(ks[9], (DEPTH, H_RET), 0.1),
        'w_ret_o': nrm(ks[10], (DEPTH, RET_V_W, D_MODEL), RET_V_W ** -0.5),
        'conv_dw_w': nrm(ks[11], (DEPTH, CONV_K, CONV_W), CONV_K ** -0.5),
        'conv_dw_b': nrm(ks[12], (DEPTH, CONV_W), 0.01),
        'conv_ln_g': 1.0 + nrm(ks[13], (DEPTH, CONV_W), 0.01),
        'conv_ln_b': nrm(ks[14], (DEPTH, CONV_W), 0.01),
        'w_conv_o': nrm(ks[15], (DEPTH, CONV_W, D_MODEL), CONV_W ** -0.5),
        'w_out': nrm(ks[16], (DEPTH, D_MODEL, D_MODEL), D_MODEL ** -0.5),
        'norm2_g': 1.0 + nrm(ks[17], (DEPTH, D_MODEL), 0.01),
        'router_w': nrm(ks[18], (DEPTH, D_MODEL, N_EXPERTS), D_MODEL ** -0.5),
        'router_b': nrm(ks[19], (DEPTH, N_EXPERTS), 0.01),
        'exp_w_gate': nrm(ks[20], (DEPTH, N_EXPERTS, D_MODEL, D_EXPERT), D_MODEL ** -0.5),
        'exp_b_gate': nrm(ks[21], (DEPTH, N_EXPERTS, D_EXPERT), 0.01),
        'exp_w_up': nrm(ks[22], (DEPTH, N_EXPERTS, D_MODEL, D_EXPERT), D_MODEL ** -0.5),
        'exp_b_up': nrm(ks[23], (DEPTH, N_EXPERTS, D_EXPERT), 0.01),
        'exp_w_down': nrm(ks[24], (DEPTH, N_EXPERTS, D_EXPERT, D_MODEL), D_EXPERT ** -0.5),
        'exp_b_down': nrm(ks[25], (DEPTH, N_EXPERTS, D_MODEL), 0.01),
        'final_norm_g': 1.0 + nrm(ks[26], (D_MODEL,), 0.01),
    }


def reference(x, c, ctx, c_ctx, ada_w, ada_b, norm1_g, w_in, ret_decay_fwd, ret_decay_bwd, w_ret_o,
              conv_dw_w, conv_dw_b, conv_ln_g, conv_ln_b, w_conv_o, w_out, norm2_g, router_w, router_b,
              exp_w_gate, exp_b_gate, exp_w_up, exp_b_up, exp_w_down, exp_b_down, final_norm_g):
    n_lat = x.shape[1]
    n_ctx = ctx.shape[1]
    ROWS = n_lat // GRID_W
    rows = jnp.repeat(jnp.arange(ROWS, dtype=jnp.float32), GRID_W)
    cols = jnp.tile(jnp.arange(GRID_W, dtype=jnp.float32), ROWS)
    cond = jax.nn.silu(jnp.concatenate([c, c_ctx[None, :]], axis=0))
    z = jnp.concatenate([ctx, x], axis=1)
    nc = n_ctx
    for l in range(DEPTH):
        mod = cond @ ada_w[l] + ada_b[l]
        ml = jnp.split(mod[:-1], N_MOD, axis=-1)
        mc = jnp.split(mod[-1], N_MOD, axis=-1)
        h = _modulate(_rmsnorm(z, norm1_g[l]), nc, mc[0], mc[1], ml[0], ml[1])
        y = _mixer(h, nc, rows, cols, w_in[l], ret_decay_fwd[l], ret_decay_bwd[l], w_ret_o[l],
                   conv_dw_w[l], conv_dw_b[l], conv_ln_g[l], conv_ln_b[l], w_conv_o[l], w_out[l])
        z = _gated_residual(z, y, nc, mc[2], ml[2])
        if l == DEPTH - 1:
            z = z[:, nc:]
            nc = 0
        h = _modulate(_rmsnorm(z, norm2_g[l]), nc, mc[3], mc[4], ml[3], ml[4])
        y = _moe(h, router_w[l], router_b[l], exp_w_gate[l], exp_b_gate[l], exp_w_up[l], exp_b_up[l],
                 exp_w_down[l], exp_b_down[l])
        z = _gated_residual(z, y, nc, mc[5], ml[5])
    return _rmsnorm(z, final_norm_g)
```

```python
import functools

import jax
import jax.numpy as jnp
from jax import lax
from jax.experimental import pallas as pl
from jax.experimental.pallas import tpu as pltpu

F32 = jnp.float32
BF16 = jnp.bfloat16

GRID_W = 64
H_RET = 8
RET_D = 128
RET_W = H_RET * RET_D
ROPE_THETA = 10000.0
CONV_K = 31
CONV_HALO = 16
TOP_K = 4
SWIGLU_LIMIT = 7.0
SWIGLU_ALPHA = 1.702
N_MOD = 6
RMS_EPS = 1e-6
LN_EPS = 1e-5
MOD_ROWS = 16

CHUNK = 256
ROW_SUB = 256
VMEM_LIMIT = 56 * 1024 * 1024


def _cparams(sem):
    return pltpu.CompilerParams(dimension_semantics=sem, vmem_limit_bytes=VMEM_LIMIT)


def _pick_tile(n, pref):
    t = min(n, pref)
    while n % t:
        t //= 2
    return t


def _ada_kernel(cond_ref, w_ref, b_ref, o_ref):
    c = cond_ref[...]
    s = (c * jax.nn.sigmoid(c)).astype(BF16)
    o_ref[...] = jnp.dot(s, w_ref[...].astype(BF16), preferred_element_type=F32) + b_ref[...]


def _ada_call(cond16, ada_w, ada_b):
    depth, d, nw = ada_w.shape
    tn = _pick_tile(nw, 1024)
    return pl.pallas_call(
        _ada_kernel,
        grid=(depth, nw // tn),
        in_specs=[
            pl.BlockSpec((MOD_ROWS, d), lambda l, j: (0, 0)),
            pl.BlockSpec((None, d, tn), lambda l, j: (l, 0, j)),
            pl.BlockSpec((None, 1, tn), lambda l, j: (l, 0, j)),
        ],
        out_specs=pl.BlockSpec((None, MOD_ROWS, tn), lambda l, j: (l, 0, j)),
        out_shape=jax.ShapeDtypeStruct((depth, MOD_ROWS, nw), F32),
        compiler_params=_cparams(("parallel", "parallel")),
        name="ada_mod",
    )(cond16, ada_w, ada_b.reshape(depth, 1, nw))


def _norm_mod(x, g, shift, scale):
    ms = jnp.mean(x * x, axis=-1, keepdims=True)
    y = x * lax.rsqrt(ms + RMS_EPS) * g
    return y * (1.0 + scale) + shift


def _inproj_kernel(z_ref, g_ref, mod_ref, w_ref, o_ref, h_scr, *, tm):
    @pl.when(pl.program_id(1) == 0)
    def _():
        g = g_ref[...]
        shift = mod_ref[0:1, :]
        scale = mod_ref[1:2, :]

        def body(r, carry):
            rows = pl.ds(pl.multiple_of(r * ROW_SUB, ROW_SUB), ROW_SUB)
            h_scr[rows, :] = _norm_mod(z_ref[rows, :], g, shift, scale).astype(BF16)
            return carry

        lax.fori_loop(0, tm // ROW_SUB, body, 0)

    o_ref[...] = jnp.dot(h_scr[...], w_ref[...], preferred_element_type=F32).astype(o_ref.dtype)


def _inproj_call(z, g, mod, w, *, mod_row, tm):
    n, d = z.shape
    nw = w.shape[1]
    tn = _pick_tile(nw, 1024)
    return pl.pallas_call(
        functools.partial(_inproj_kernel, tm=tm),
        grid=(n // tm, nw // tn),
        in_specs=[
            pl.BlockSpec((tm, d), lambda i, j: (i, 0)),
            pl.BlockSpec((1, d), lambda i, j: (0, 0)),
            pl.BlockSpec((None, N_MOD, d), lambda i, j: (mod_row(i * tm), 0, 0)),
            pl.BlockSpec((d, tn), lambda i, j: (0, j)),
        ],
        out_specs=pl.BlockSpec((tm, tn), lambda i, j: (i, j)),
        out_shape=jax.ShapeDtypeStruct((n, nw), BF16),
        scratch_shapes=[pltpu.VMEM((tm, d), BF16)],
        compiler_params=_cparams(("parallel", "arbitrary")),
        name="mixer_in_proj",
    )(z, g.reshape(1, d), mod, w)


def _rope(x, cos_t, sin_t):
    lane = lax.broadcasted_iota(jnp.int32, x.shape, 1)
    partner = jnp.where((lane % 64) < 32, pltpu.roll(x, 96, axis=1), pltpu.roll(x, 32, axis=1))
    return x * cos_t + partner * sin_t


def _ret_fwd_kernel(dec_ref, k_ref, v_ref, cos_ref, sin_ref, zf_ref, s_out_ref, s_scr):
    @pl.when(pl.program_id(1) == 0)
    def _():
        s_scr[...] = jnp.zeros_like(s_scr)

    cos_t = cos_ref[...]
    sin_t = sin_ref[...]
    for h in range(H_RET):
        cols = slice(h * RET_D, (h + 1) * RET_D)
        s_prev = s_scr[h]
        s_out_ref[h] = s_prev
        k = _rope(k_ref[:, cols].astype(F32), cos_t, sin_t) * (RET_D ** -0.5)
        kz = (k * zf_ref[h]).astype(BF16)
        ds = lax.dot_general(kz, v_ref[:, cols], (((0,), (0,)), ((), ())), preferred_element_type=F32)
        s_scr[h] = dec_ref[0, h] * s_prev + ds


def _ret_bwd_kernel(dec_ref, q_ref, k_ref, v_ref, g_ref, cos_ref, sin_ref, dm_ref, tab_ref, sf_ref,
                    o_ref, s_scr):
    @pl.when(pl.program_id(1) == 0)
    def _():
        s_scr[...] = jnp.zeros_like(s_scr)

    cos_t = cos_ref[...]
    sin_t = sin_ref[...]
    for h in range(H_RET):
        cols = slice(h * RET_D, (h + 1) * RET_D)
        q = _rope(q_ref[:, cols].astype(F32), cos_t, sin_t)
        k = _rope(k_ref[:, cols].astype(F32), cos_t, sin_t) * (RET_D ** -0.5)
        v = v_ref[:, cols]
        s = lax.dot_general(q.astype(BF16), k.astype(BF16), (((1,), (1,)), ((), ())),
                            preferred_element_type=F32)
        a = (s * dm_ref[h]).astype(BF16)
        sb = s_scr[h]
        o = jnp.dot(a, v, preferred_element_type=F32)
        o += jnp.dot((q * tab_ref[h, 0]).astype(BF16), sf_ref[h].astype(BF16), preferred_element_type=F32)
        o += jnp.dot((q * tab_ref[h, 1]).astype(BF16), sb.astype(BF16), preferred_element_type=F32)
        kz = (k * tab_ref[h, 2]).astype(BF16)
        ds = lax.dot_general(kz, v, (((0,), (0,)), ((), ())), preferred_element_type=F32)
        s_scr[h] = dec_ref[1, h] * sb + ds
        mu = jnp.mean(o, axis=-1, keepdims=True)
        oc = o - mu
        var = jnp.mean(oc * oc, axis=-1, keepdims=True)
        gate = g_ref[:, cols].astype(F32)
        o_ref[:, cols] = (oc * lax.rsqrt(var + LN_EPS) * (gate * jax.nn.sigmoid(gate))).astype(o_ref.dtype)


def _retention_call(p, dec, cos_t, sin_t, dmask, tab, *, batch, seq, ctx):
    n = p.shape[0]
    nl = seq // CHUNK
    nlb = batch * nl
    steps = nl + 1
    assert ctx == CHUNK

    def fwd_rows(b, s):
        return jnp.where(s == 0, nlb + b, b * nl + s - 1)

    def fwd_pos(b, s):
        return jnp.where(s == 0, nl, s - 1)

    def bwd_rows(b, s):
        return jnp.where(s == 0, nlb + b, b * nl + nl - s)

    def bwd_pos(b, s):
        return jnp.where(s == 0, nl, nl - s)

    def bwd_state(b, s):
        return jnp.where(s == 0, 0, nl - s + 1)

    smem = pl.BlockSpec(memory_space=pltpu.SMEM)
    s_prev = pl.pallas_call(
        _ret_fwd_kernel,
        grid=(batch, steps),
        in_specs=[
            smem,
            pl.BlockSpec((CHUNK, RET_W), lambda b, s: (fwd_rows(b, s), 1)),
            pl.BlockSpec((CHUNK, RET_W), lambda b, s: (fwd_rows(b, s), 2)),
            pl.BlockSpec((CHUNK, RET_D), lambda b, s: (fwd_pos(b, s), 0)),
            pl.BlockSpec((CHUNK, RET_D), lambda b, s: (fwd_pos(b, s), 0)),
            pl.BlockSpec((H_RET, CHUNK, RET_D), lambda b, s: (0, 0, 0)),
        ],
        out_specs=pl.BlockSpec((None, None, H_RET, RET_D, RET_D), lambda b, s: (b, s, 0, 0, 0)),
        out_shape=jax.ShapeDtypeStruct((batch, steps, H_RET, RET_D, RET_D), F32),
        scratch_shapes=[pltpu.VMEM((H_RET, RET_D, RET_D), F32)],
        compiler_params=_cparams(("parallel", "arbitrary")),
        name="retention_fwd_state",
    )(dec, p, p, cos_t, sin_t, tab[:, 3])

    return pl.pallas_call(
        _ret_bwd_kernel,
        grid=(batch, steps),
        in_specs=[
            smem,
            pl.BlockSpec((CHUNK, RET_W), lambda b, s: (bwd_rows(b, s), 0)),
            pl.BlockSpec((CHUNK, RET_W), lambda b, s: (bwd_rows(b, s), 1)),
            pl.BlockSpec((CHUNK, RET_W), lambda b, s: (bwd_rows(b, s), 2)),
            pl.BlockSpec((CHUNK, RET_W), lambda b, s: (bwd_rows(b, s), 3)),
            pl.BlockSpec((CHUNK, RET_D), lambda b, s: (bwd_pos(b, s), 0)),
            pl.BlockSpec((CHUNK, RET_D), lambda b, s: (bwd_pos(b, s), 0)),
            pl.BlockSpec((H_RET, CHUNK, CHUNK), lambda b, s: (0, 0, 0)),
            pl.BlockSpec((H_RET, 3, CHUNK, RET_D), lambda b, s: (0, 0, 0, 0)),
            pl.BlockSpec((None, None, H_RET, RET_D, RET_D), lambda b, s: (b, bwd_state(b, s), 0, 0, 0)),
        ],
        out_specs=pl.BlockSpec((CHUNK, RET_W), lambda b, s: (bwd_rows(b, s), 0)),
        out_shape=jax.ShapeDtypeStruct((n, RET_W), BF16),
        scratch_shapes=[pltpu.VMEM((H_RET, RET_D, RET_D), F32)],
        compiler_params=_cparams(("parallel", "arbitrary")),
        name="retention_out",
    )(dec, p, p, p, p, cos_t, sin_t, dmask, tab[:, :3], s_prev)


def _retention_tables(dec_f, dec_b):
    lgf = jax.nn.log_sigmoid(dec_f.astype(F32))[:, None, None]
    lgb = jax.nn.log_sigmoid(dec_b.astype(F32))[:, None, None]
    i = jnp.arange(CHUNK, dtype=F32)
    diff = i[:, None] - i[None, :]
    dmask = jnp.where(diff >= 0, jnp.exp(lgf * jnp.maximum(diff, 0.0)), jnp.exp(lgb * jnp.maximum(-diff, 0.0)))
    col = i[None, :, None]
    ones = jnp.ones((1, 1, RET_D), F32)
    xi_f = jnp.exp(lgf * (col + 1.0)) * ones
    xi_b = jnp.exp(lgb * (CHUNK - col)) * ones
    zeta_b = jnp.exp(lgb * col) * ones
    zeta_f = jnp.exp(lgf * (CHUNK - 1.0 - col)) * ones
    tab = jnp.stack([xi_f, xi_b, zeta_b, zeta_f], axis=1)
    dec = jnp.stack([jnp.exp(lgf[:, 0, 0] * CHUNK), jnp.exp(lgb[:, 0, 0] * CHUNK)])
    return dmask, tab, dec


def _rope_tables(seq, ctx):
    quarter = RET_D // 4
    inv = ROPE_THETA ** (-jnp.arange(quarter, dtype=F32) / quarter)
    t = jnp.arange(seq)
    rows = (t // GRID_W).astype(F32)
    cols = (t % GRID_W).astype(F32)
    ar = rows[:, None] * inv[None, :]
    ac = cols[:, None] * inv[None, :]
    cos_t = jnp.concatenate([jnp.cos(ar), jnp.cos(ar), jnp.cos(ac), jnp.cos(ac)], axis=1)
    sin_t = jnp.concatenate([-jnp.sin(ar), jnp.sin(ar), -jnp.sin(ac), jnp.sin(ac)], axis=1)
    cos_t = jnp.concatenate([cos_t, jnp.ones((ctx, RET_D), F32)], axis=0)
    sin_t = jnp.concatenate([sin_t, jnp.zeros((ctx, RET_D), F32)], axis=0)
    return cos_t, sin_t


CONV_ROWS = 64
CONV_LANES = 128


def _glu(a, b):
    a = a.astype(F32)
    b = b.astype(F32)
    return a * jax.nn.sigmoid(b)


def _conv_kernel(a_ref, b_ref, ap_ref, bp_ref, an_ref, bn_ref, w_ref, cb_ref, lg_ref, lb_ref,
                 o_ref, u_scr, y_scr, *, tc, blocks_per_seq, n_lat_blocks):
    i = pl.program_id(0)
    is_ctx = i >= n_lat_blocks
    first = jnp.logical_or(is_ctx, i % blocks_per_seq == 0)
    last = jnp.logical_or(is_ctx, i % blocks_per_seq == blocks_per_seq - 1)
    cw = a_ref.shape[1]
    u_scr[CONV_HALO:CONV_HALO + tc, :] = _glu(a_ref[...], b_ref[...])
    u_scr[0:CONV_HALO, :] = jnp.where(first, 0.0, _glu(ap_ref[...], bp_ref[...]))
    u_scr[CONV_HALO + tc:, :] = jnp.where(last, 0.0, _glu(an_ref[...], bn_ref[...]))

    base = CONV_HALO - CONV_K // 2

    def lane_tile(t, carry):
        lanes = pl.ds(pl.multiple_of(t * CONV_LANES, CONV_LANES), CONV_LANES)
        for r0 in range(0, tc, CONV_ROWS):
            acc = jnp.zeros((CONV_ROWS, CONV_LANES), F32)
            for kk in range(CONV_K):
                acc = acc + u_scr[r0 + base + kk:r0 + base + kk + CONV_ROWS, lanes] * w_ref[kk:kk + 1, lanes]
            y_scr[r0:r0 + CONV_ROWS, lanes] = acc
        return carry

    lax.fori_loop(0, cw // CONV_LANES, lane_tile, 0)

    y = y_scr[...] + cb_ref[...]
    mu = jnp.mean(y, axis=-1, keepdims=True)
    yc = y - mu
    var = jnp.mean(yc * yc, axis=-1, keepdims=True)
    yn = yc * lax.rsqrt(var + LN_EPS) * lg_ref[...] + lb_ref[...]
    o_ref[...] = (yn * jax.nn.sigmoid(yn)).astype(o_ref.dtype)


def _conv_call(p, w, cb, lg, lb, *, batch, seq, ctx, col_a, col_b):
    n = p.shape[0]
    cw = w.shape[1]
    tc = 256
    assert seq % tc == 0 and ctx == tc
    hb = tc // CONV_HALO
    n_halo = n // CONV_HALO
    n_lat_blocks = batch * seq // tc

    def prev(i):
        return jnp.maximum(i * hb - 1, 0)

    def nxt(i):
        return jnp.minimum((i + 1) * hb, n_halo - 1)

    vec = lambda: pl.BlockSpec((1, cw), lambda i: (0, 0))
    return pl.pallas_call(
        functools.partial(_conv_kernel, tc=tc, blocks_per_seq=seq // tc, n_lat_blocks=n_lat_blocks),
        grid=(n // tc,),
        in_specs=[
            pl.BlockSpec((tc, cw), lambda i: (i, col_a)),
            pl.BlockSpec((tc, cw), lambda i: (i, col_b)),
            pl.BlockSpec((CONV_HALO, cw), lambda i: (prev(i), col_a)),
            pl.BlockSpec((CONV_HALO, cw), lambda i: (prev(i), col_b)),
            pl.BlockSpec((CONV_HALO, cw), lambda i: (nxt(i), col_a)),
            pl.BlockSpec((CONV_HALO, cw), lambda i: (nxt(i), col_b)),
            pl.BlockSpec((CONV_K, cw), lambda i: (0, 0)),
            vec(), vec(), vec(),
        ],
        out_specs=pl.BlockSpec((tc, cw), lambda i: (i, 0)),
        out_shape=jax.ShapeDtypeStruct((n, cw), BF16),
        scratch_shapes=[pltpu.VMEM((tc + 2 * CONV_HALO, cw), F32), pltpu.VMEM((tc, cw), F32)],
        compiler_params=_cparams(("parallel",)),
        name="conv_branch",
    )(p, p, p, p, p, p, w, cb.reshape(1, cw), lg.reshape(1, cw), lb.reshape(1, cw))


def _mixout_kernel(r_ref, u_ref, sr_ref, sc_ref, z_ref, mod_ref, wr_ref, wc_ref, wo_ref, o_ref):
    y_ret = jnp.dot(r_ref[...], wr_ref[...], preferred_element_type=F32)
    y_conv = jnp.dot(u_ref[...], wc_ref[...], preferred_element_type=F32)
    merged = (jax.nn.sigmoid(sr_ref[...].astype(F32)) * y_ret
              + jax.nn.sigmoid(sc_ref[...].astype(F32)) * y_conv)
    y = jnp.dot(merged.astype(BF16), wo_ref[...], preferred_element_type=F32)
    o_ref[...] = z_ref[...] + mod_ref[2:3, :] * y


def _mixout_call(r, u, p, z, mod, wr, wc, wo, *, mod_row, col_sr, col_sc):
    n, d = z.shape
    tm = 256
    const = lambda shape: pl.BlockSpec(shape, lambda i: (0, 0), pipeline_mode=pl.Buffered(1))
    return pl.pallas_call(
        _mixout_kernel,
        grid=(n // tm,),
        in_specs=[
            pl.BlockSpec((tm, r.shape[1]), lambda i: (i, 0)),
            pl.BlockSpec((tm, u.shape[1]), lambda i: (i, 0)),
            pl.BlockSpec((tm, d), lambda i: (i, col_sr)),
            pl.BlockSpec((tm, d), lambda i: (i, col_sc)),
            pl.BlockSpec((tm, d), lambda i: (i, 0)),
            pl.BlockSpec((None, N_MOD, d), lambda i: (mod_row(i * tm), 0, 0)),
            const(wr.shape), const(wc.shape), const(wo.shape),
        ],
        out_specs=pl.BlockSpec((tm, d), lambda i: (i, 0)),
        out_shape=jax.ShapeDtypeStruct((n, d), F32),
        compiler_params=_cparams(("parallel",)),
        name="mixer_out_proj",
    )(r, u, p, p, z, mod, wr, wc, wo)


def _router_kernel(z_ref, g_ref, mod_ref, rw_ref, rb_ref, h_ref, idx_ref, gate_ref, *, tm, n_exp):
    g = g_ref[...]
    shift = mod_ref[3:4, :]
    scale = mod_ref[4:5, :]
    rw = rw_ref[...]
    rb = rb_ref[...]

    def body(r, carry):
        r0 = pl.multiple_of(r * ROW_SUB, ROW_SUB)
        rows = pl.ds(r0, ROW_SUB)
        h = _norm_mod(z_ref[rows, :], g, shift, scale).astype(BF16)
        h_ref[rows, :] = h
        lg = lax.dot_general(rw, h, (((1,), (1,)), ((), ())), preferred_element_type=F32) + rb
        eidx = lax.broadcasted_iota(jnp.int32, lg.shape, 0)
        vals, sels = [], []
        work = lg
        for _ in range(TOP_K):
            m = jnp.max(work, axis=0, keepdims=True)
            sel = jnp.min(jnp.where(work == m, eidx, n_exp), axis=0, keepdims=True)
            vals.append(m)
            sels.append(sel)
            work = jnp.where(eidx == sel, -jnp.inf, work)
        ex = [jnp.exp(v - vals[0]) for v in vals]
        tot = ex[0] + ex[1] + ex[2] + ex[3]
        for s in range(TOP_K):
            idx_ref[s:s + 1, rows] = sels[s]
            gate_ref[s:s + 1, rows] = ex[s] / tot
        return carry

    lax.fori_loop(0, tm // ROW_SUB, body, 0)


def _router_call(z, n_rows, g, mod, rw_t, rb, *, mod_row, tm):
    d = z.shape[1]
    n_exp = rw_t.shape[0]
    return pl.pallas_call(
        functools.partial(_router_kernel, tm=tm, n_exp=n_exp),
        grid=(n_rows // tm,),
        in_specs=[
            pl.BlockSpec((tm, d), lambda i: (i, 0)),
            pl.BlockSpec((1, d), lambda i: (0, 0)),
            pl.BlockSpec((None, N_MOD, d), lambda i: (mod_row(i * tm), 0, 0)),
            pl.BlockSpec((n_exp, d), lambda i: (0, 0)),
            pl.BlockSpec((n_exp, 1), lambda i: (0, 0)),
        ],
        out_specs=[
            pl.BlockSpec((tm, d), lambda i: (i, 0)),
            pl.BlockSpec((TOP_K, tm), lambda i: (0, i)),
            pl.BlockSpec((TOP_K, tm), lambda i: (0, i)),
        ],
        out_shape=[
            jax.ShapeDtypeStruct((n_rows, d), BF16),
            jax.ShapeDtypeStruct((TOP_K, n_rows), jnp.int32),
            jax.ShapeDtypeStruct((TOP_K, n_rows), F32),
        ],
        compiler_params=_cparams(("parallel",)),
        name="moe_router",
    )(z, g.reshape(1, d), mod, rw_t, rb.reshape(n_exp, 1))


EXP_TILE = 256


def _experts_kernel(te_ref, nu_ref, x_ref, wg_ref, bg_ref, wu_ref, bu_ref, wd_ref, bd_ref, o_ref):
    @pl.when(pl.program_id(0) < nu_ref[0])
    def _():
        x = x_ref[...]
        a = jnp.minimum(jnp.dot(x, wg_ref[...], preferred_element_type=F32) + bg_ref[...], SWIGLU_LIMIT)
        b = jnp.clip(jnp.dot(x, wu_ref[...], preferred_element_type=F32) + bu_ref[...],
                     -SWIGLU_LIMIT, SWIGLU_LIMIT)
        act = (b + 1.0) * a * jax.nn.sigmoid(SWIGLU_ALPHA * a)
        y = jnp.dot(act.astype(BF16), wd_ref[...], preferred_element_type=F32) + bd_ref[...]
        o_ref[...] = y.astype(o_ref.dtype)

    @pl.when(pl.program_id(0) >= nu_ref[0])
    def _():
        o_ref[...] = jnp.zeros_like(o_ref)


def _experts_call(tile_expert, n_used, xs, wg, bg, wu, bu, wd, bd):
    rows, d = xs.shape
    n_exp, _, de = wg.shape
    grid_spec = pltpu.PrefetchScalarGridSpec(
        num_scalar_prefetch=2,
        grid=(rows // EXP_TILE,),
        in_specs=[
            pl.BlockSpec((EXP_TILE, d), lambda j, te, nu: (j, 0)),
            pl.BlockSpec((None, d, de), lambda j, te, nu: (te[j], 0, 0)),
            pl.BlockSpec((None, 1, de), lambda j, te, nu: (te[j], 0, 0)),
            pl.BlockSpec((None, d, de), lambda j, te, nu: (te[j], 0, 0)),
            pl.BlockSpec((None, 1, de), lambda j, te, nu: (te[j], 0, 0)),
            pl.BlockSpec((None, de, d), lambda j, te, nu: (te[j], 0, 0)),
            pl.BlockSpec((None, 1, d), lambda j, te, nu: (te[j], 0, 0)),
        ],
        out_specs=pl.BlockSpec((EXP_TILE, d), lambda j, te, nu: (j, 0)),
    )
    return pl.pallas_call(
        _experts_kernel,
        grid_spec=grid_spec,
        out_shape=jax.ShapeDtypeStruct((rows, d), BF16),
        compiler_params=_cparams(("arbitrary",)),
        name="moe_experts",
    )(tile_expert, n_used, xs, wg, bg.reshape(n_exp, 1, de), wu, bu.reshape(n_exp, 1, de),
      wd, bd.reshape(n_exp, 1, d))


def _combine_kernel(y_ref, p_ref, z_ref, mod_ref, fg_ref, o_ref, *, final):
    p = p_ref[...]
    acc = p[:, 0:1] * y_ref[0].astype(F32)
    for s in range(1, TOP_K):
        acc += p[:, s:s + 1] * y_ref[s].astype(F32)
    z = z_ref[...] + mod_ref[5:6, :] * acc
    if final:
        ms = jnp.mean(z * z, axis=-1, keepdims=True)
        z = z * lax.rsqrt(ms + RMS_EPS) * fg_ref[...]
    o_ref[...] = z


def _combine_call(yg, gates_t, z, n_rows, mod, fg, *, mod_row, final):
    d = z.shape[1]
    tm = 256
    return pl.pallas_call(
        functools.partial(_combine_kernel, final=final),
        grid=(n_rows // tm,),
        in_specs=[
            pl.BlockSpec((TOP_K, tm, d), lambda i: (0, i, 0)),
            pl.BlockSpec((tm, TOP_K), lambda i: (i, 0)),
            pl.BlockSpec((tm, d), lambda i: (i, 0)),
            pl.BlockSpec((None, N_MOD, d), lambda i: (mod_row(i * tm), 0, 0)),
            pl.BlockSpec((1, d), lambda i: (0, 0)),
        ],
        out_specs=pl.BlockSpec((tm, d), lambda i: (i, 0)),
        out_shape=jax.ShapeDtypeStruct((n_rows, d), F32),
        compiler_params=_cparams(("parallel",)),
        name="moe_combine",
    )(yg, gates_t, z, mod, fg.reshape(1, d))


def _dispatch_plan(idx, n_exp):
    k, n = idx.shape
    e_flat = idx.reshape(-1)
    onehot = (e_flat[:, None] == jnp.arange(n_exp, dtype=jnp.int32)[None, :]).astype(jnp.int32)
    csum = jnp.cumsum(onehot, axis=0)
    rank = jnp.sum(csum * onehot, axis=1) - 1
    counts = csum[-1]
    padded = ((counts + EXP_TILE - 1) // EXP_TILE) * EXP_TILE
    ends = jnp.cumsum(padded)
    offsets = ends - padded
    pos = offsets[e_flat] + rank
    rows = k * n + n_exp * EXP_TILE
    tok = jnp.tile(jnp.arange(n, dtype=jnp.int32), k)
    row_token = jnp.zeros((rows,), jnp.int32).at[pos].set(tok)
    n_tiles = rows // EXP_TILE
    tile_start = jnp.arange(n_tiles, dtype=jnp.int32) * EXP_TILE
    n_used = (ends[-1] // EXP_TILE).astype(jnp.int32)
    tile_expert = jnp.searchsorted(ends, tile_start, side="right").astype(jnp.int32)
    last_expert = tile_expert[jnp.maximum(n_used - 1, 0)]
    tile_expert = jnp.where(jnp.arange(n_tiles) < n_used, jnp.minimum(tile_expert, n_exp - 1), last_expert)
    return pos.reshape(k, n), row_token, tile_expert, n_used.reshape(1)


def kernel(x, c, ctx, c_ctx, ada_w, ada_b, norm1_g, w_in, ret_decay_fwd, ret_decay_bwd, w_ret_o,
           conv_dw_w, conv_dw_b, conv_ln_g, conv_ln_b, w_conv_o, w_out, norm2_g, router_w, router_b,
           exp_w_gate, exp_b_gate, exp_w_up, exp_b_up, exp_w_down, exp_b_down, final_norm_g):
    batch, seq, d = x.shape
    n_ctx = ctx.shape[1]
    depth = ada_w.shape[0]
    n_exp = router_w.shape[2]
    n_lat = batch * seq
    assert batch + 1 <= MOD_ROWS and seq % CHUNK == 0 and n_ctx == CHUNK
    tm_big = _pick_tile(seq, 1024)
    while (batch * n_ctx) % tm_big:
        tm_big //= 2

    def mod_row(row0):
        return jnp.where(row0 >= n_lat, batch, row0 // seq)

    cond = jnp.concatenate([c, c_ctx[None, :], jnp.zeros((MOD_ROWS - batch - 1, d), F32)], axis=0)
    mod_all = _ada_call(cond, ada_w, ada_b).reshape(depth, MOD_ROWS, N_MOD, d)

    z = jnp.concatenate([x.reshape(n_lat, d), ctx.reshape(batch * n_ctx, d)], axis=0)
    cos_t, sin_t = _rope_tables(seq, n_ctx)

    for l in range(depth):
        mod = mod_all[l]
        p = _inproj_call(z, norm1_g[l], mod, w_in[l].astype(BF16), mod_row=mod_row, tm=tm_big)
        dmask, tab, dec = _retention_tables(ret_decay_fwd[l], ret_decay_bwd[l])
        r = _retention_call(p, dec, cos_t, sin_t, dmask, tab, batch=batch, seq=seq, ctx=n_ctx)
        cw = conv_dw_w.shape[2]
        col0 = 4 * RET_W // cw
        u = _conv_call(p, conv_dw_w[l], conv_dw_b[l], conv_ln_g[l], conv_ln_b[l],
                       batch=batch, seq=seq, ctx=n_ctx, col_a=col0, col_b=col0 + 1)
        s_col = (4 * RET_W + 2 * cw) // d
        z = _mixout_call(r, u, p, z, mod, w_ret_o[l].astype(BF16), w_conv_o[l].astype(BF16),
                         w_out[l].astype(BF16), mod_row=mod_row, col_sr=s_col, col_sc=s_col + 1)
        last = l == depth - 1
        n_rows = n_lat if last else z.shape[0]
        h, idx, gates = _router_call(z, n_rows, norm2_g[l], mod, router_w[l].T.astype(BF16), router_b[l],
                                     mod_row=mod_row, tm=tm_big)
        pos, row_token, tile_expert, n_used = _dispatch_plan(idx, n_exp)
        xs = jnp.take(h, row_token, axis=0)
        ys = _experts_call(tile_expert, n_used, xs, exp_w_gate[l].astype(BF16), exp_b_gate[l],
                           exp_w_up[l].astype(BF16), exp_b_up[l], exp_w_down[l].astype(BF16), exp_b_down[l])
        yg = jnp.take(ys, pos.reshape(-1), axis=0).reshape(TOP_K, n_rows, d)
        z = _combine_call(yg, gates.T, z, n_rows, mod, final_norm_g, mod_row=mod_row, final=last)
    return z.reshape(batch, seq, d)
```

```python
import functools

import jax
import jax.numpy as jnp
from jax import lax
from jax.experimental import pallas as pl
from jax.experimental.pallas import tpu as pltpu

F32 = jnp.float32
BF16 = jnp.bfloat16

GRID_W = 64
H_RET = 8
RET_D = 128
RET_W = H_RET * RET_D
ROPE_THETA = 10000.0
CONV_K = 31
CONV_HALO = 16
TOP_K = 4
SWIGLU_LIMIT = 7.0
SWIGLU_ALPHA = 1.702
N_MOD = 6
RMS_EPS = 1e-6
LN_EPS = 1e-5
MOD_ROWS = 16

CHUNK = 256
ROW_SUB = 256
VMEM_LIMIT = 56 * 1024 * 1024


def _cparams(sem):
    return pltpu.CompilerParams(dimension_semantics=sem, vmem_limit_bytes=VMEM_LIMIT)


def _pick_tile(n, pref):
    t = min(n, pref)
    while n % t:
        t //= 2
    return t


def _sigmoid(x):
    return 0.5 * jnp.tanh(0.5 * x) + 0.5


def _ada_kernel(cond_ref, w_ref, b_ref, o_ref):
    c = cond_ref[...]
    s = (c * _sigmoid(c)).astype(BF16)
    o_ref[...] = jnp.dot(s, w_ref[...].astype(BF16), preferred_element_type=F32) + b_ref[...]


def _ada_call(cond16, ada_w, ada_b):
    depth, d, nw = ada_w.shape
    tn = _pick_tile(nw, 1024)
    return pl.pallas_call(
        _ada_kernel,
        grid=(depth, nw // tn),
        in_specs=[
            pl.BlockSpec((MOD_ROWS, d), lambda l, j: (0, 0)),
            pl.BlockSpec((None, d, tn), lambda l, j: (l, 0, j)),
            pl.BlockSpec((None, 1, tn), lambda l, j: (l, 0, j)),
        ],
        out_specs=pl.BlockSpec((None, MOD_ROWS, tn), lambda l, j: (l, 0, j)),
        out_shape=jax.ShapeDtypeStruct((depth, MOD_ROWS, nw), F32),
        compiler_params=_cparams(("parallel", "parallel")),
        name="ada_mod",
    )(cond16, ada_w, ada_b.reshape(depth, 1, nw))


def _norm_mod(x, g, shift, scale):
    ms = jnp.mean(x * x, axis=-1, keepdims=True)
    y = x * lax.rsqrt(ms + RMS_EPS) * g
    return y * (1.0 + scale) + shift


def _inproj_kernel(z_ref, g_ref, mod_ref, w_ref, o_ref, h_scr, *, tm):
    @pl.when(pl.program_id(1) == 0)
    def _():
        g = g_ref[...]
        shift = mod_ref[0:1, :]
        scale = mod_ref[1:2, :]

        def body(r, carry):
            rows = pl.ds(pl.multiple_of(r * ROW_SUB, ROW_SUB), ROW_SUB)
            h_scr[rows, :] = _norm_mod(z_ref[rows, :], g, shift, scale).astype(BF16)
            return carry

        lax.fori_loop(0, tm // ROW_SUB, body, 0)

    o_ref[...] = jnp.dot(h_scr[...], w_ref[...], preferred_element_type=F32).astype(o_ref.dtype)


def _inproj_call(z, g, mod, w, *, mod_row, tm):
    n, d = z.shape
    nw = w.shape[1]
    tn = _pick_tile(nw, 1024)
    return pl.pallas_call(
        functools.partial(_inproj_kernel, tm=tm),
        grid=(n // tm, nw // tn),
        in_specs=[
            pl.BlockSpec((tm, d), lambda i, j: (i, 0)),
            pl.BlockSpec((1, d), lambda i, j: (0, 0)),
            pl.BlockSpec((None, N_MOD, d), lambda i, j: (mod_row(i * tm), 0, 0)),
            pl.BlockSpec((d, tn), lambda i, j: (0, j)),
        ],
        out_specs=pl.BlockSpec((tm, tn), lambda i, j: (i, j)),
        out_shape=jax.ShapeDtypeStruct((n, nw), BF16),
        scratch_shapes=[pltpu.VMEM((tm, d), BF16)],
        compiler_params=_cparams(("parallel", "arbitrary")),
        name="mixer_in_proj",
    )(z, g.reshape(1, d), mod, w)


def _rope(x, cos_t, sin_t):
    lane = lax.broadcasted_iota(jnp.int32, x.shape, 1)
    partner = jnp.where((lane % 64) < 32, pltpu.roll(x, 96, axis=1), pltpu.roll(x, 32, axis=1))
    return x * cos_t + partner * sin_t


def _ret_fwd_kernel(dec_ref, k_ref, v_ref, cos_ref, sin_ref, zf_ref, s_out_ref, s_scr):
    @pl.when(pl.program_id(1) == 0)
    def _():
        s_scr[...] = jnp.zeros_like(s_scr)

    cos_t = cos_ref[...]
    sin_t = sin_ref[...]
    for h in range(H_RET):
        cols = slice(h * RET_D, (h + 1) * RET_D)
        s_prev = s_scr[h]
        s_out_ref[h] = s_prev
        k = _rope(k_ref[:, cols].astype(F32), cos_t, sin_t) * (RET_D ** -0.5)
        kz = (k * zf_ref[h]).astype(BF16)
        ds = lax.dot_general(kz, v_ref[:, cols], (((0,), (0,)), ((), ())), preferred_element_type=F32)
        s_scr[h] = dec_ref[0, h] * s_prev + ds


def _ret_bwd_kernel(dec_ref, q_ref, k_ref, v_ref, g_ref, cos_ref, sin_ref, dm_ref, tab_ref, sf_ref,
                    o_ref, s_scr):
    @pl.when(pl.program_id(1) == 0)
    def _():
        s_scr[...] = jnp.zeros_like(s_scr)

    cos_t = cos_ref[...]
    sin_t = sin_ref[...]
    for h in range(H_RET):
        cols = slice(h * RET_D, (h + 1) * RET_D)
        q = _rope(q_ref[:, cols].astype(F32), cos_t, sin_t)
        k = _rope(k_ref[:, cols].astype(F32), cos_t, sin_t) * (RET_D ** -0.5)
        v = v_ref[:, cols]
        s = lax.dot_general(q.astype(BF16), k.astype(BF16), (((1,), (1,)), ((), ())),
                            preferred_element_type=F32)
        a = (s * dm_ref[h]).astype(BF16)
        sb = s_scr[h]
        o = jnp.dot(a, v, preferred_element_type=F32)
        o += jnp.dot((q * tab_ref[h, 0]).astype(BF16), sf_ref[h].astype(BF16), preferred_element_type=F32)
        o += jnp.dot((q * tab_ref[h, 1]).astype(BF16), sb.astype(BF16), preferred_element_type=F32)
        kz = (k * tab_ref[h, 2]).astype(BF16)
        ds = lax.dot_general(kz, v, (((0,), (0,)), ((), ())), preferred_element_type=F32)
        s_scr[h] = dec_ref[1, h] * sb + ds
        mu = jnp.mean(o, axis=-1, keepdims=True)
        oc = o - mu
        var = jnp.mean(oc * oc, axis=-1, keepdims=True)
        gate = g_ref[:, cols].astype(F32)
        o_ref[:, cols] = (oc * lax.rsqrt(var + LN_EPS) * (gate * _sigmoid(gate))).astype(o_ref.dtype)


def _retention_call(p, dec, cos_t, sin_t, dmask, tab, *, batch, seq, ctx):
    n = p.shape[0]
    nl = seq // CHUNK
    nlb = batch * nl
    steps = nl + 1
    assert ctx == CHUNK

    def fwd_rows(b, s):
        return jnp.where(s == 0, nlb + b, b * nl + s - 1)

    def fwd_pos(b, s):
        return jnp.where(s == 0, nl, s - 1)

    def bwd_rows(b, s):
        return jnp.where(s == 0, nlb + b, b * nl + nl - s)

    def bwd_pos(b, s):
        return jnp.where(s == 0, nl, nl - s)

    def bwd_state(b, s):
        return jnp.where(s == 0, 0, nl - s + 1)

    smem = pl.BlockSpec(memory_space=pltpu.SMEM)
    s_prev = pl.pallas_call(
        _ret_fwd_kernel,
        grid=(batch, steps),
        in_specs=[
            smem,
            pl.BlockSpec((CHUNK, RET_W), lambda b, s: (fwd_rows(b, s), 1)),
            pl.BlockSpec((CHUNK, RET_W), lambda b, s: (fwd_rows(b, s), 2)),
            pl.BlockSpec((CHUNK, RET_D), lambda b, s: (fwd_pos(b, s), 0)),
            pl.BlockSpec((CHUNK, RET_D), lambda b, s: (fwd_pos(b, s), 0)),
            pl.BlockSpec((H_RET, CHUNK, RET_D), lambda b, s: (0, 0, 0)),
        ],
        out_specs=pl.BlockSpec((None, None, H_RET, RET_D, RET_D), lambda b, s: (b, s, 0, 0, 0)),
        out_shape=jax.ShapeDtypeStruct((batch, steps, H_RET, RET_D, RET_D), F32),
        scratch_shapes=[pltpu.VMEM((H_RET, RET_D, RET_D), F32)],
        compiler_params=_cparams(("parallel", "arbitrary")),
        name="retention_fwd_state",
    )(dec, p, p, cos_t, sin_t, tab[:, 3])

    return pl.pallas_call(
        _ret_bwd_kernel,
        grid=(batch, steps),
        in_specs=[
            smem,
            pl.BlockSpec((CHUNK, RET_W), lambda b, s: (bwd_rows(b, s), 0)),
            pl.BlockSpec((CHUNK, RET_W), lambda b, s: (bwd_rows(b, s), 1)),
            pl.BlockSpec((CHUNK, RET_W), lambda b, s: (bwd_rows(b, s), 2)),
            pl.BlockSpec((CHUNK, RET_W), lambda b, s: (bwd_rows(b, s), 3)),
            pl.BlockSpec((CHUNK, RET_D), lambda b, s: (bwd_pos(b, s), 0)),
            pl.BlockSpec((CHUNK, RET_D), lambda b, s: (bwd_pos(b, s), 0)),
            pl.BlockSpec((H_RET, CHUNK, CHUNK), lambda b, s: (0, 0, 0)),
            pl.BlockSpec((H_RET, 3, CHUNK, RET_D), lambda b, s: (0, 0, 0, 0)),
            pl.BlockSpec((None, None, H_RET, RET_D, RET_D), lambda b, s: (b, bwd_state(b, s), 0, 0, 0)),
        ],
        out_specs=pl.BlockSpec((CHUNK, RET_W), lambda b, s: (bwd_rows(b, s), 0)),
        out_shape=jax.ShapeDtypeStruct((n, RET_W), BF16),
        scratch_shapes=[pltpu.VMEM((H_RET, RET_D, RET_D), F32)],
        compiler_params=_cparams(("parallel", "arbitrary")),
        name="retention_out",
    )(dec, p, p, p, p, cos_t, sin_t, dmask, tab[:, :3], s_prev)


def _retention_tables(dec_f, dec_b):
    lgf = jax.nn.log_sigmoid(dec_f.astype(F32))[:, None, None]
    lgb = jax.nn.log_sigmoid(dec_b.astype(F32))[:, None, None]
    i = jnp.arange(CHUNK, dtype=F32)
    diff = i[:, None] - i[None, :]
    dmask = jnp.where(diff >= 0, jnp.exp(lgf * jnp.maximum(diff, 0.0)), jnp.exp(lgb * jnp.maximum(-diff, 0.0)))
    col = i[None, :, None]
    ones = jnp.ones((1, 1, RET_D), F32)
    xi_f = jnp.exp(lgf * (col + 1.0)) * ones
    xi_b = jnp.exp(lgb * (CHUNK - col)) * ones
    zeta_b = jnp.exp(lgb * col) * ones
    zeta_f = jnp.exp(lgf * (CHUNK - 1.0 - col)) * ones
    tab = jnp.stack([xi_f, xi_b, zeta_b, zeta_f], axis=1)
    dec = jnp.stack([jnp.exp(lgf[:, 0, 0] * CHUNK), jnp.exp(lgb[:, 0, 0] * CHUNK)])
    return dmask, tab, dec


def _rope_tables(seq, ctx):
    quarter = RET_D // 4
    inv = ROPE_THETA ** (-jnp.arange(quarter, dtype=F32) / quarter)
    t = jnp.arange(seq)
    rows = (t // GRID_W).astype(F32)
    cols = (t % GRID_W).astype(F32)
    ar = rows[:, None] * inv[None, :]
    ac = cols[:, None] * inv[None, :]
    cos_t = jnp.concatenate([jnp.cos(ar), jnp.cos(ar), jnp.cos(ac), jnp.cos(ac)], axis=1)
    sin_t = jnp.concatenate([-jnp.sin(ar), jnp.sin(ar), -jnp.sin(ac), jnp.sin(ac)], axis=1)
    cos_t = jnp.concatenate([cos_t, jnp.ones((ctx, RET_D), F32)], axis=0)
    sin_t = jnp.concatenate([sin_t, jnp.zeros((ctx, RET_D), F32)], axis=0)
    return cos_t, sin_t


CONV_ROWS = 128
CONV_LANES = 128


def _glu(a, b):
    a = a.astype(F32)
    b = b.astype(F32)
    return a * _sigmoid(b)


def _conv_kernel(a_ref, b_ref, ap_ref, bp_ref, an_ref, bn_ref, w_ref, cb_ref, lg_ref, lb_ref,
                 o_ref, u_scr, y_scr, *, tc, blocks_per_seq, n_lat_blocks):
    i = pl.program_id(0)
    is_ctx = i >= n_lat_blocks
    first = jnp.logical_or(is_ctx, i % blocks_per_seq == 0)
    last = jnp.logical_or(is_ctx, i % blocks_per_seq == blocks_per_seq - 1)
    cw = a_ref.shape[1]
    u_scr[CONV_HALO:CONV_HALO + tc, :] = _glu(a_ref[...], b_ref[...])
    u_scr[0:CONV_HALO, :] = jnp.where(first, 0.0, _glu(ap_ref[...], bp_ref[...]))
    u_scr[CONV_HALO + tc:, :] = jnp.where(last, 0.0, _glu(an_ref[...], bn_ref[...]))

    base = CONV_HALO - CONV_K // 2

    def lane_tile(t, carry):
        lanes = pl.ds(pl.multiple_of(t * CONV_LANES, CONV_LANES), CONV_LANES)
        for r0 in range(0, tc, CONV_ROWS):
            strip = u_scr[r0:r0 + CONV_ROWS + 2 * CONV_HALO, lanes]
            acc = jnp.zeros((CONV_ROWS, CONV_LANES), F32)
            for phase in range(8):
                shifted = strip if phase == 0 else pltpu.roll(strip, strip.shape[0] - phase, axis=0)
                for kk in range(CONV_K):
                    if (base + kk) % 8 == phase:
                        lo = base + kk - phase
                        acc = acc + shifted[lo:lo + CONV_ROWS] * w_ref[kk:kk + 1, lanes]
            y_scr[r0:r0 + CONV_ROWS, lanes] = acc
        return carry

    lax.fori_loop(0, cw // CONV_LANES, lane_tile, 0)

    y = y_scr[...] + cb_ref[...]
    mu = jnp.mean(y, axis=-1, keepdims=True)
    yc = y - mu
    var = jnp.mean(yc * yc, axis=-1, keepdims=True)
    yn = yc * lax.rsqrt(var + LN_EPS) * lg_ref[...] + lb_ref[...]
    o_ref[...] = (yn * _sigmoid(yn)).astype(o_ref.dtype)


def _conv_call(p, w, cb, lg, lb, *, batch, seq, ctx, col_a, col_b):
    n = p.shape[0]
    cw = w.shape[1]
    tc = 256
    assert seq % tc == 0 and ctx == tc
    hb = tc // CONV_HALO
    n_halo = n // CONV_HALO
    n_lat_blocks = batch * seq // tc

    def prev(i):
        return jnp.maximum(i * hb - 1, 0)

    def nxt(i):
        return jnp.minimum((i + 1) * hb, n_halo - 1)

    vec = lambda: pl.BlockSpec((1, cw), lambda i: (0, 0))
    return pl.pallas_call(
        functools.partial(_conv_kernel, tc=tc, blocks_per_seq=seq // tc, n_lat_blocks=n_lat_blocks),
        grid=(n // tc,),
        in_specs=[
            pl.BlockSpec((tc, cw), lambda i: (i, col_a)),
            pl.BlockSpec((tc, cw), lambda i: (i, col_b)),
            pl.BlockSpec((CONV_HALO, cw), lambda i: (prev(i), col_a)),
            pl.BlockSpec((CONV_HALO, cw), lambda i: (prev(i), col_b)),
            pl.BlockSpec((CONV_HALO, cw), lambda i: (nxt(i), col_a)),
            pl.BlockSpec((CONV_HALO, cw), lambda i: (nxt(i), col_b)),
            pl.BlockSpec((CONV_K, cw), lambda i: (0, 0)),
            vec(), vec(), vec(),
        ],
        out_specs=pl.BlockSpec((tc, cw), lambda i: (i, 0)),
        out_shape=jax.ShapeDtypeStruct((n, cw), BF16),
        scratch_shapes=[pltpu.VMEM((tc + 2 * CONV_HALO, cw), F32), pltpu.VMEM((tc, cw), F32)],
        compiler_params=_cparams(("parallel",)),
        name="conv_branch",
    )(p, p, p, p, p, p, w, cb.reshape(1, cw), lg.reshape(1, cw), lb.reshape(1, cw))


def _mixout_kernel(r_ref, u_ref, sr_ref, sc_ref, z_ref, mod_ref, wr_ref, wc_ref, wo_ref, o_ref):
    y_ret = jnp.dot(r_ref[...], wr_ref[...], preferred_element_type=F32)
    y_conv = jnp.dot(u_ref[...], wc_ref[...], preferred_element_type=F32)
    merged = (_sigmoid(sr_ref[...].astype(F32)) * y_ret
              + _sigmoid(sc_ref[...].astype(F32)) * y_conv)
    y = jnp.dot(merged.astype(BF16), wo_ref[...], preferred_element_type=F32)
    o_ref[...] = z_ref[...] + mod_ref[2:3, :] * y


def _mixout_call(r, u, p, z, mod, wr, wc, wo, *, mod_row, col_sr, col_sc):
    n, d = z.shape
    tm = 256
    const = lambda shape: pl.BlockSpec(shape, lambda i: (0, 0), pipeline_mode=pl.Buffered(1))
    return pl.pallas_call(
        _mixout_kernel,
        grid=(n // tm,),
        in_specs=[
            pl.BlockSpec((tm, r.shape[1]), lambda i: (i, 0)),
            pl.BlockSpec((tm, u.shape[1]), lambda i: (i, 0)),
            pl.BlockSpec((tm, d), lambda i: (i, col_sr)),
            pl.BlockSpec((tm, d), lambda i: (i, col_sc)),
            pl.BlockSpec((tm, d), lambda i: (i, 0)),
            pl.BlockSpec((None, N_MOD, d), lambda i: (mod_row(i * tm), 0, 0)),
            const(wr.shape), const(wc.shape), const(wo.shape),
        ],
        out_specs=pl.BlockSpec((tm, d), lambda i: (i, 0)),
        out_shape=jax.ShapeDtypeStruct((n, d), F32),
        compiler_params=_cparams(("parallel",)),
        name="mixer_out_proj",
    )(r, u, p, p, z, mod, wr, wc, wo)


def _router_kernel(z_ref, g_ref, mod_ref, rw_ref, rb_ref, tri_ref, h_ref, idx_ref, gate_ref, rank_ref, cnt_ref,
                   base_scr, *, tm, n_exp):
    @pl.when(pl.program_id(0) == 0)
    def _():
        base_scr[...] = jnp.zeros_like(base_scr)

    g = g_ref[...]
    shift = mod_ref[3:4, :]
    scale = mod_ref[4:5, :]
    rw = rw_ref[...]
    rb = rb_ref[...]
    tri = tri_ref[...]

    def body(r, carry):
        r0 = pl.multiple_of(r * ROW_SUB, ROW_SUB)
        rows = pl.ds(r0, ROW_SUB)
        h = _norm_mod(z_ref[rows, :], g, shift, scale).astype(BF16)
        h_ref[rows, :] = h
        lg = lax.dot_general(rw, h, (((1,), (1,)), ((), ())), preferred_element_type=F32) + rb
        eidx = lax.broadcasted_iota(jnp.int32, lg.shape, 0)
        vals, hits = [], []
        work = lg
        for _ in range(TOP_K):
            m = jnp.max(work, axis=0, keepdims=True)
            sel = jnp.min(jnp.where(work == m, eidx, n_exp), axis=0, keepdims=True)
            hit = eidx == sel
            vals.append(m)
            hits.append(hit)
            work = jnp.where(hit, -jnp.inf, work)
        ex = [jnp.exp(v - vals[0]) for v in vals]
        tot = ex[0] + ex[1] + ex[2] + ex[3]
        chosen = jnp.where(hits[0] | hits[1] | hits[2] | hits[3], 1.0, 0.0)
        before = jnp.dot(chosen.astype(BF16), tri, preferred_element_type=F32) + base_scr[:, 0:1]
        for s in range(TOP_K):
            idx_ref[s:s + 1, rows] = jnp.sum(jnp.where(hits[s], eidx, 0), axis=0, keepdims=True)
            gate_ref[s:s + 1, rows] = ex[s] / tot
            rank_ref[s:s + 1, rows] = jnp.sum(jnp.where(hits[s], before, 0.0), axis=0,
                                              keepdims=True).astype(jnp.int32)
        base_scr[...] += jnp.sum(chosen, axis=1, keepdims=True)
        return carry

    lax.fori_loop(0, tm // ROW_SUB, body, 0)
    cnt_ref[...] = base_scr[...].astype(jnp.int32)


def _router_call(z, n_rows, g, mod, rw_t, rb, *, mod_row, tm):
    d = z.shape[1]
    n_exp = rw_t.shape[0]
    tri = jnp.triu(jnp.ones((ROW_SUB, ROW_SUB), BF16), k=1)
    tok = lambda: pl.BlockSpec((TOP_K, tm), lambda i: (0, i))
    return pl.pallas_call(
        functools.partial(_router_kernel, tm=tm, n_exp=n_exp),
        grid=(n_rows // tm,),
        in_specs=[
            pl.BlockSpec((tm, d), lambda i: (i, 0)),
            pl.BlockSpec((1, d), lambda i: (0, 0)),
            pl.BlockSpec((None, N_MOD, d), lambda i: (mod_row(i * tm), 0, 0)),
            pl.BlockSpec((n_exp, d), lambda i: (0, 0)),
            pl.BlockSpec((n_exp, 1), lambda i: (0, 0)),
            pl.BlockSpec((ROW_SUB, ROW_SUB), lambda i: (0, 0)),
        ],
        out_specs=[
            pl.BlockSpec((tm, d), lambda i: (i, 0)),
            tok(), tok(), tok(),
            pl.BlockSpec((n_exp, 128), lambda i: (0, 0)),
        ],
        out_shape=[
            jax.ShapeDtypeStruct((n_rows, d), BF16),
            jax.ShapeDtypeStruct((TOP_K, n_rows), jnp.int32),
            jax.ShapeDtypeStruct((TOP_K, n_rows), F32),
            jax.ShapeDtypeStruct((TOP_K, n_rows), jnp.int32),
            jax.ShapeDtypeStruct((n_exp, 128), jnp.int32),
        ],
        scratch_shapes=[pltpu.VMEM((n_exp, 128), F32)],
        compiler_params=_cparams(("arbitrary",)),
        name="moe_router",
    )(z, g.reshape(1, d), mod, rw_t, rb.reshape(n_exp, 1), tri)


EXP_TILE = 256


def _cast_rows(src_ref, dst_ref):
    n = src_ref.shape[0]
    step = min(n, ROW_SUB)

    def body(r, carry):
        rows = pl.ds(pl.multiple_of(r * step, step), step)
        dst_ref[rows, :] = src_ref[rows, :].astype(dst_ref.dtype)
        return carry

    lax.fori_loop(0, n // step, body, 0)


def _experts_kernel(te_ref, nu_ref, x_ref, wg_ref, bg_ref, wu_ref, bu_ref, wd_ref, bd_ref, o_ref,
                    wg_s, wu_s, wd_s):
    j = pl.program_id(0)
    active = j < nu_ref[0]
    new_expert = jnp.logical_or(j == 0, te_ref[j] != te_ref[jnp.maximum(j - 1, 0)])

    @pl.when(jnp.logical_and(active, new_expert))
    def _():
        _cast_rows(wg_ref, wg_s)
        _cast_rows(wu_ref, wu_s)
        _cast_rows(wd_ref, wd_s)

    @pl.when(active)
    def _():
        x = x_ref[...]
        a = jnp.minimum(jnp.dot(x, wg_s[...], preferred_element_type=F32) + bg_ref[...], SWIGLU_LIMIT)
        b = jnp.clip(jnp.dot(x, wu_s[...], preferred_element_type=F32) + bu_ref[...],
                     -SWIGLU_LIMIT, SWIGLU_LIMIT)
        act = (b + 1.0) * a * _sigmoid(SWIGLU_ALPHA * a)
        y = jnp.dot(act.astype(BF16), wd_s[...], preferred_element_type=F32) + bd_ref[...]
        o_ref[...] = y.astype(o_ref.dtype)

    @pl.when(jnp.logical_not(active))
    def _():
        o_ref[...] = jnp.zeros_like(o_ref)


def _experts_call(tile_expert, n_used, xs, wg, bg, wu, bu, wd, bd):
    rows, d = xs.shape
    n_exp, _, de = wg.shape
    grid_spec = pltpu.PrefetchScalarGridSpec(
        num_scalar_prefetch=2,
        grid=(rows // EXP_TILE,),
        in_specs=[
            pl.BlockSpec((EXP_TILE, d), lambda j, te, nu: (j, 0)),
            pl.BlockSpec((None, d, de), lambda j, te, nu: (te[j], 0, 0)),
            pl.BlockSpec((None, 1, de), lambda j, te, nu: (te[j], 0, 0)),
            pl.BlockSpec((None, d, de), lambda j, te, nu: (te[j], 0, 0)),
            pl.BlockSpec((None, 1, de), lambda j, te, nu: (te[j], 0, 0)),
            pl.BlockSpec((None, de, d), lambda j, te, nu: (te[j], 0, 0)),
            pl.BlockSpec((None, 1, d), lambda j, te, nu: (te[j], 0, 0)),
        ],
        out_specs=pl.BlockSpec((EXP_TILE, d), lambda j, te, nu: (j, 0)),
        scratch_shapes=[pltpu.VMEM((d, de), BF16), pltpu.VMEM((d, de), BF16), pltpu.VMEM((de, d), BF16)],
    )
    return pl.pallas_call(
        _experts_kernel,
        grid_spec=grid_spec,
        out_shape=jax.ShapeDtypeStruct((rows, d), BF16),
        compiler_params=_cparams(("arbitrary",)),
        name="moe_experts",
    )(tile_expert, n_used, xs, wg, bg.reshape(n_exp, 1, de), wu, bu.reshape(n_exp, 1, de),
      wd, bd.reshape(n_exp, 1, d))


def _combine_kernel(y_ref, p_ref, z_ref, mod_ref, fg_ref, o_ref, *, final):
    p = p_ref[...]
    acc = p[:, 0:1] * y_ref[0].astype(F32)
    for s in range(1, TOP_K):
        acc += p[:, s:s + 1] * y_ref[s].astype(F32)
    z = z_ref[...] + mod_ref[5:6, :] * acc
    if final:
        ms = jnp.mean(z * z, axis=-1, keepdims=True)
        z = z * lax.rsqrt(ms + RMS_EPS) * fg_ref[...]
    o_ref[...] = z


def _combine_call(yg, gates_t, z, n_rows, mod, fg, *, mod_row, final):
    d = z.shape[1]
    tm = 256
    return pl.pallas_call(
        functools.partial(_combine_kernel, final=final),
        grid=(n_rows // tm,),
        in_specs=[
            pl.BlockSpec((TOP_K, tm, d), lambda i: (0, i, 0)),
            pl.BlockSpec((tm, TOP_K), lambda i: (i, 0)),
            pl.BlockSpec((tm, d), lambda i: (i, 0)),
            pl.BlockSpec((None, N_MOD, d), lambda i: (mod_row(i * tm), 0, 0)),
            pl.BlockSpec((1, d), lambda i: (0, 0)),
        ],
        out_specs=pl.BlockSpec((tm, d), lambda i: (i, 0)),
        out_shape=jax.ShapeDtypeStruct((n_rows, d), F32),
        compiler_params=_cparams(("parallel",)),
        name="moe_combine",
    )(yg, gates_t, z, mod, fg.reshape(1, d))


def _dispatch_plan(idx, rank, counts):
    k, n = idx.shape
    n_exp = counts.shape[0]
    padded = ((counts + EXP_TILE - 1) // EXP_TILE) * EXP_TILE
    ends = jnp.cumsum(padded)
    offsets = ends - padded
    onehot = idx[:, :, None] == jnp.arange(n_exp, dtype=jnp.int32)[None, None, :]
    pos = (rank + jnp.sum(jnp.where(onehot, offsets[None, None, :], 0), axis=-1)).reshape(-1)
    rows = k * n + n_exp * EXP_TILE
    tok = jnp.tile(jnp.arange(n, dtype=jnp.int32), k)
    row_token = jnp.zeros((rows,), jnp.int32).at[pos].set(tok, unique_indices=True, mode="promise_in_bounds")
    n_tiles = rows // EXP_TILE
    tile_start = jnp.arange(n_tiles, dtype=jnp.int32) * EXP_TILE
    n_used = (ends[-1] // EXP_TILE).astype(jnp.int32)
    tile_expert = jnp.searchsorted(ends, tile_start, side="right").astype(jnp.int32)
    last_expert = tile_expert[jnp.maximum(n_used - 1, 0)]
    tile_expert = jnp.where(jnp.arange(n_tiles) < n_used, jnp.minimum(tile_expert, n_exp - 1), last_expert)
    return pos.reshape(k, n), row_token, tile_expert, n_used.reshape(1)


def kernel(x, c, ctx, c_ctx, ada_w, ada_b, norm1_g, w_in, ret_decay_fwd, ret_decay_bwd, w_ret_o,
           conv_dw_w, conv_dw_b, conv_ln_g, conv_ln_b, w_conv_o, w_out, norm2_g, router_w, router_b,
           exp_w_gate, exp_b_gate, exp_w_up, exp_b_up, exp_w_down, exp_b_down, final_norm_g):
    batch, seq, d = x.shape
    n_ctx = ctx.shape[1]
    depth = ada_w.shape[0]
    n_exp = router_w.shape[2]
    assert batch + 1 <= MOD_ROWS and seq % CHUNK == 0 and n_ctx == CHUNK
    n_groups = 2 if batch % 2 == 0 else 1
    gb = batch // n_groups
    n_lat = gb * seq
    tm_big = _pick_tile(seq, 1024)
    while (gb * n_ctx) % tm_big:
        tm_big //= 2

    def make_mod_row(g):
        return lambda row0: jnp.where(row0 >= n_lat, batch, g * gb + row0 // seq)

    mod_rows = [make_mod_row(g) for g in range(n_groups)]
    cond = jnp.concatenate([c, c_ctx[None, :], jnp.zeros((MOD_ROWS - batch - 1, d), F32)], axis=0)
    mod_all = _ada_call(cond, ada_w, ada_b).reshape(depth, MOD_ROWS, N_MOD, d)
    cos_t, sin_t = _rope_tables(seq, n_ctx)
    cw = conv_dw_w.shape[2]
    col0 = 4 * RET_W // cw
    s_col = (4 * RET_W + 2 * cw) // d

    zs = [jnp.concatenate([x[g * gb:(g + 1) * gb].reshape(n_lat, d),
                           ctx[g * gb:(g + 1) * gb].reshape(gb * n_ctx, d)], axis=0) for g in range(n_groups)]

    for l in range(depth):
        mod = mod_all[l]
        last = l == depth - 1
        w_in_l = w_in[l].astype(BF16)
        w_ret_l = w_ret_o[l].astype(BF16)
        w_conv_l = w_conv_o[l].astype(BF16)
        w_out_l = w_out[l].astype(BF16)
        rw_t = router_w[l].T.astype(BF16)
        dmask, tab, dec = _retention_tables(ret_decay_fwd[l], ret_decay_bwd[l])
        staged = []
        for g in range(n_groups):
            z, mod_row = zs[g], mod_rows[g]
            p = _inproj_call(z, norm1_g[l], mod, w_in_l, mod_row=mod_row, tm=tm_big)
            r = _retention_call(p, dec, cos_t, sin_t, dmask, tab, batch=gb, seq=seq, ctx=n_ctx)
            u = _conv_call(p, conv_dw_w[l], conv_dw_b[l], conv_ln_g[l], conv_ln_b[l],
                           batch=gb, seq=seq, ctx=n_ctx, col_a=col0, col_b=col0 + 1)
            z = _mixout_call(r, u, p, z, mod, w_ret_l, w_conv_l, w_out_l, mod_row=mod_row,
                             col_sr=s_col, col_sc=s_col + 1)
            n_rows = n_lat if last else z.shape[0]
            h, idx, gates, rank, counts = _router_call(z, n_rows, norm2_g[l], mod, rw_t, router_b[l],
                                                       mod_row=mod_row, tm=tm_big)
            pos, row_token, tile_expert, n_used = _dispatch_plan(idx, rank, counts[:, 0])
            xs = h.at[row_token].get(mode="promise_in_bounds")
            staged.append((z, n_rows, gates, pos, tile_expert, n_used, xs))
        gathered = []
        for g in range(n_groups):
            z, n_rows, gates, pos, tile_expert, n_used, xs = staged[g]
            ys = _experts_call(tile_expert, n_used, xs, exp_w_gate[l], exp_b_gate[l], exp_w_up[l], exp_b_up[l],
                               exp_w_down[l], exp_b_down[l])
            gathered.append(ys.at[pos].get(mode="promise_in_bounds").reshape(TOP_K, n_rows, d))
        for g in range(n_groups):
            z, n_rows, gates = staged[g][:3]
            zs[g] = _combine_call(gathered[g], gates.T, z, n_rows, mod, final_norm_g, mod_row=mod_rows[g],
                                  final=last)
    return jnp.concatenate([z.reshape(gb, seq, d) for z in zs], axis=0)
```

```python
import functools

import jax
import jax.numpy as jnp
from jax import lax
from jax.experimental import pallas as pl
from jax.experimental.pallas import tpu as pltpu

F32 = jnp.float32
BF16 = jnp.bfloat16

GRID_W = 64
H_RET = 8
RET_D = 128
RET_W = H_RET * RET_D
ROPE_THETA = 10000.0
CONV_K = 31
CONV_HALO = 16
TOP_K = 4
SWIGLU_LIMIT = 7.0
SWIGLU_ALPHA = 1.702
N_MOD = 6
RMS_EPS = 1e-6
LN_EPS = 1e-5
MOD_ROWS = 16

CHUNK = 256
ROW_SUB = 256
VMEM_LIMIT = 56 * 1024 * 1024


def _cparams(sem):
    return pltpu.CompilerParams(dimension_semantics=sem, vmem_limit_bytes=VMEM_LIMIT)


def _pick_tile(n, pref):
    t = min(n, pref)
    while n % t:
        t //= 2
    return t


def _sigmoid(x):
    return 0.5 * jnp.tanh(0.5 * x) + 0.5


def _ada_kernel(cond_ref, w_ref, b_ref, o_ref):
    c = cond_ref[...]
    s = (c * _sigmoid(c)).astype(BF16)
    o_ref[...] = jnp.dot(s, w_ref[...].astype(BF16), preferred_element_type=F32) + b_ref[...]


def _ada_call(cond16, ada_w, ada_b):
    depth, d, nw = ada_w.shape
    tn = _pick_tile(nw, 1024)
    return pl.pallas_call(
        _ada_kernel,
        grid=(depth, nw // tn),
        in_specs=[
            pl.BlockSpec((MOD_ROWS, d), lambda l, j: (0, 0)),
            pl.BlockSpec((None, d, tn), lambda l, j: (l, 0, j)),
            pl.BlockSpec((None, 1, tn), lambda l, j: (l, 0, j)),
        ],
        out_specs=pl.BlockSpec((None, MOD_ROWS, tn), lambda l, j: (l, 0, j)),
        out_shape=jax.ShapeDtypeStruct((depth, MOD_ROWS, nw), F32),
        compiler_params=_cparams(("parallel", "parallel")),
        name="ada_mod",
    )(cond16, ada_w, ada_b.reshape(depth, 1, nw))


def _norm_mod(x, g, shift, scale):
    ms = jnp.mean(x * x, axis=-1, keepdims=True)
    y = x * lax.rsqrt(ms + RMS_EPS) * g
    return y * (1.0 + scale) + shift


def _inproj_kernel(z_ref, g_ref, mod_ref, w_ref, o_ref, h_scr, *, tm):
    @pl.when(pl.program_id(1) == 0)
    def _():
        g = g_ref[...]
        shift = mod_ref[0:1, :]
        scale = mod_ref[1:2, :]

        def body(r, carry):
            rows = pl.ds(pl.multiple_of(r * ROW_SUB, ROW_SUB), ROW_SUB)
            h_scr[rows, :] = _norm_mod(z_ref[rows, :], g, shift, scale).astype(BF16)
            return carry

        lax.fori_loop(0, tm // ROW_SUB, body, 0)

    o_ref[...] = jnp.dot(h_scr[...], w_ref[...], preferred_element_type=F32).astype(o_ref.dtype)


def _inproj_call(z, g, mod, w, *, mod_row, tm):
    n, d = z.shape
    nw = w.shape[1]
    tn = _pick_tile(nw, 1024)
    return pl.pallas_call(
        functools.partial(_inproj_kernel, tm=tm),
        grid=(n // tm, nw // tn),
        in_specs=[
            pl.BlockSpec((tm, d), lambda i, j: (i, 0)),
            pl.BlockSpec((1, d), lambda i, j: (0, 0)),
            pl.BlockSpec((None, N_MOD, d), lambda i, j: (mod_row(i * tm), 0, 0)),
            pl.BlockSpec((d, tn), lambda i, j: (0, j)),
        ],
        out_specs=pl.BlockSpec((tm, tn), lambda i, j: (i, j)),
        out_shape=jax.ShapeDtypeStruct((n, nw), BF16),
        scratch_shapes=[pltpu.VMEM((tm, d), BF16)],
        compiler_params=_cparams(("parallel", "arbitrary")),
        name="mixer_in_proj",
    )(z, g.reshape(1, d), mod, w)


def _rope(x, cos_t, sin_t):
    lane = lax.broadcasted_iota(jnp.int32, x.shape, 1)
    partner = jnp.where((lane % 64) < 32, pltpu.roll(x, 96, axis=1), pltpu.roll(x, 32, axis=1))
    return x * cos_t + partner * sin_t


def _ret_fwd_kernel(dec_ref, k_ref, v_ref, cos_ref, sin_ref, zf_ref, s_out_ref, s_scr):
    @pl.when(pl.program_id(1) == 0)
    def _():
        s_scr[...] = jnp.zeros_like(s_scr)

    cos_t = cos_ref[...]
    sin_t = sin_ref[...]
    for h in range(H_RET):
        cols = slice(h * RET_D, (h + 1) * RET_D)
        s_prev = s_scr[h]
        s_out_ref[h] = s_prev
        k = _rope(k_ref[:, cols].astype(F32), cos_t, sin_t) * (RET_D ** -0.5)
        kz = (k * zf_ref[h]).astype(BF16)
        ds = lax.dot_general(kz, v_ref[:, cols], (((0,), (0,)), ((), ())), preferred_element_type=F32)
        s_scr[h] = dec_ref[0, h] * s_prev + ds


def _ret_bwd_kernel(dec_ref, q_ref, k_ref, v_ref, g_ref, cos_ref, sin_ref, dm_ref, tab_ref, sf_ref,
                    o_ref, s_scr):
    @pl.when(pl.program_id(1) == 0)
    def _():
        s_scr[...] = jnp.zeros_like(s_scr)

    cos_t = cos_ref[...]
    sin_t = sin_ref[...]
    for h in range(H_RET):
        cols = slice(h * RET_D, (h + 1) * RET_D)
        q = _rope(q_ref[:, cols].astype(F32), cos_t, sin_t)
        k = _rope(k_ref[:, cols].astype(F32), cos_t, sin_t) * (RET_D ** -0.5)
        v = v_ref[:, cols]
        s = lax.dot_general(q.astype(BF16), k.astype(BF16), (((1,), (1,)), ((), ())),
                            preferred_element_type=F32)
        a = (s * dm_ref[h]).astype(BF16)
        sb = s_scr[h]
        o = jnp.dot(a, v, preferred_element_type=F32)
        o += jnp.dot((q * tab_ref[h, 0]).astype(BF16), sf_ref[h].astype(BF16), preferred_element_type=F32)
        o += jnp.dot((q * tab_ref[h, 1]).astype(BF16), sb.astype(BF16), preferred_element_type=F32)
        kz = (k * tab_ref[h, 2]).astype(BF16)
        ds = lax.dot_general(kz, v, (((0,), (0,)), ((), ())), preferred_element_type=F32)
        s_scr[h] = dec_ref[1, h] * sb + ds
        mu = jnp.mean(o, axis=-1, keepdims=True)
        oc = o - mu
        var = jnp.mean(oc * oc, axis=-1, keepdims=True)
        gate = g_ref[:, cols].astype(F32)
        o_ref[:, cols] = (oc * lax.rsqrt(var + LN_EPS) * (gate * _sigmoid(gate))).astype(o_ref.dtype)


def _retention_call(p, dec, cos_t, sin_t, dmask, tab, *, batch, seq, ctx):
    n = p.shape[0]
    nl = seq // CHUNK
    nlb = batch * nl
    steps = nl + 1
    assert ctx == CHUNK

    def fwd_rows(b, s):
        return jnp.where(s == 0, nlb + b, b * nl + s - 1)

    def fwd_pos(b, s):
        return jnp.where(s == 0, nl, s - 1)

    def bwd_rows(b, s):
        return jnp.where(s == 0, nlb + b, b * nl + nl - s)

    def bwd_pos(b, s):
        return jnp.where(s == 0, nl, nl - s)

    def bwd_state(b, s):
        return jnp.where(s == 0, 0, nl - s + 1)

    smem = pl.BlockSpec(memory_space=pltpu.SMEM)
    s_prev = pl.pallas_call(
        _ret_fwd_kernel,
        grid=(batch, steps),
        in_specs=[
            smem,
            pl.BlockSpec((CHUNK, RET_W), lambda b, s: (fwd_rows(b, s), 1)),
            pl.BlockSpec((CHUNK, RET_W), lambda b, s: (fwd_rows(b, s), 2)),
            pl.BlockSpec((CHUNK, RET_D), lambda b, s: (fwd_pos(b, s), 0)),
            pl.BlockSpec((CHUNK, RET_D), lambda b, s: (fwd_pos(b, s), 0)),
            pl.BlockSpec((H_RET, CHUNK, RET_D), lambda b, s: (0, 0, 0)),
        ],
        out_specs=pl.BlockSpec((None, None, H_RET, RET_D, RET_D), lambda b, s: (b, s, 0, 0, 0)),
        out_shape=jax.ShapeDtypeStruct((batch, steps, H_RET, RET_D, RET_D), F32),
        scratch_shapes=[pltpu.VMEM((H_RET, RET_D, RET_D), F32)],
        compiler_params=_cparams(("parallel", "arbitrary")),
        name="retention_fwd_state",
    )(dec, p, p, cos_t, sin_t, tab[:, 3])

    return pl.pallas_call(
        _ret_bwd_kernel,
        grid=(batch, steps),
        in_specs=[
            smem,
            pl.BlockSpec((CHUNK, RET_W), lambda b, s: (bwd_rows(b, s), 0)),
            pl.BlockSpec((CHUNK, RET_W), lambda b, s: (bwd_rows(b, s), 1)),
            pl.BlockSpec((CHUNK, RET_W), lambda b, s: (bwd_rows(b, s), 2)),
            pl.BlockSpec((CHUNK, RET_W), lambda b, s: (bwd_rows(b, s), 3)),
            pl.BlockSpec((CHUNK, RET_D), lambda b, s: (bwd_pos(b, s), 0)),
            pl.BlockSpec((CHUNK, RET_D), lambda b, s: (bwd_pos(b, s), 0)),
            pl.BlockSpec((H_RET, CHUNK, CHUNK), lambda b, s: (0, 0, 0)),
            pl.BlockSpec((H_RET, 3, CHUNK, RET_D), lambda b, s: (0, 0, 0, 0)),
            pl.BlockSpec((None, None, H_RET, RET_D, RET_D), lambda b, s: (b, bwd_state(b, s), 0, 0, 0)),
        ],
        out_specs=pl.BlockSpec((CHUNK, RET_W), lambda b, s: (bwd_rows(b, s), 0)),
        out_shape=jax.ShapeDtypeStruct((n, RET_W), BF16),
        scratch_shapes=[pltpu.VMEM((H_RET, RET_D, RET_D), F32)],
        compiler_params=_cparams(("parallel", "arbitrary")),
        name="retention_out",
    )(dec, p, p, p, p, cos_t, sin_t, dmask, tab[:, :3], s_prev)


def _retention_tables(dec_f, dec_b):
    lgf = jax.nn.log_sigmoid(dec_f.astype(F32))[:, None, None]
    lgb = jax.nn.log_sigmoid(dec_b.astype(F32))[:, None, None]
    i = jnp.arange(CHUNK, dtype=F32)
    diff = i[:, None] - i[None, :]
    dmask = jnp.where(diff >= 0, jnp.exp(lgf * jnp.maximum(diff, 0.0)), jnp.exp(lgb * jnp.maximum(-diff, 0.0)))
    col = i[None, :, None]
    ones = jnp.ones((1, 1, RET_D), F32)
    xi_f = jnp.exp(lgf * (col + 1.0)) * ones
    xi_b = jnp.exp(lgb * (CHUNK - col)) * ones
    zeta_b = jnp.exp(lgb * col) * ones
    zeta_f = jnp.exp(lgf * (CHUNK - 1.0 - col)) * ones
    tab = jnp.stack([xi_f, xi_b, zeta_b, zeta_f], axis=1)
    dec = jnp.stack([jnp.exp(lgf[:, 0, 0] * CHUNK), jnp.exp(lgb[:, 0, 0] * CHUNK)])
    return dmask, tab, dec


def _rope_tables(seq, ctx):
    quarter = RET_D // 4
    inv = ROPE_THETA ** (-jnp.arange(quarter, dtype=F32) / quarter)
    t = jnp.arange(seq)
    rows = (t // GRID_W).astype(F32)
    cols = (t % GRID_W).astype(F32)
    ar = rows[:, None] * inv[None, :]
    ac = cols[:, None] * inv[None, :]
    cos_t = jnp.concatenate([jnp.cos(ar), jnp.cos(ar), jnp.cos(ac), jnp.cos(ac)], axis=1)
    sin_t = jnp.concatenate([-jnp.sin(ar), jnp.sin(ar), -jnp.sin(ac), jnp.sin(ac)], axis=1)
    cos_t = jnp.concatenate([cos_t, jnp.ones((ctx, RET_D), F32)], axis=0)
    sin_t = jnp.concatenate([sin_t, jnp.zeros((ctx, RET_D), F32)], axis=0)
    return cos_t, sin_t


CONV_ROWS = 128
CONV_LANES = 128


def _glu(a, b):
    a = a.astype(F32)
    b = b.astype(F32)
    return a * _sigmoid(b)


def _conv_kernel(a_ref, b_ref, ap_ref, bp_ref, an_ref, bn_ref, w_ref, cb_ref, lg_ref, lb_ref,
                 o_ref, u_scr, y_scr, *, tc, blocks_per_seq, n_lat_blocks):
    i = pl.program_id(0)
    is_ctx = i >= n_lat_blocks
    first = jnp.logical_or(is_ctx, i % blocks_per_seq == 0)
    last = jnp.logical_or(is_ctx, i % blocks_per_seq == blocks_per_seq - 1)
    cw = a_ref.shape[1]
    u_scr[CONV_HALO:CONV_HALO + tc, :] = _glu(a_ref[...], b_ref[...])
    u_scr[0:CONV_HALO, :] = jnp.where(first, 0.0, _glu(ap_ref[...], bp_ref[...]))
    u_scr[CONV_HALO + tc:, :] = jnp.where(last, 0.0, _glu(an_ref[...], bn_ref[...]))

    base = CONV_HALO - CONV_K // 2

    def lane_tile(t, carry):
        lanes = pl.ds(pl.multiple_of(t * CONV_LANES, CONV_LANES), CONV_LANES)
        for r0 in range(0, tc, CONV_ROWS):
            strip = u_scr[r0:r0 + CONV_ROWS + 2 * CONV_HALO, lanes]
            acc = jnp.zeros((CONV_ROWS, CONV_LANES), F32)
            for phase in range(8):
                shifted = strip if phase == 0 else pltpu.roll(strip, strip.shape[0] - phase, axis=0)
                for kk in range(CONV_K):
                    if (base + kk) % 8 == phase:
                        lo = base + kk - phase
                        acc = acc + shifted[lo:lo + CONV_ROWS] * w_ref[kk:kk + 1, lanes]
            y_scr[r0:r0 + CONV_ROWS, lanes] = acc
        return carry

    lax.fori_loop(0, cw // CONV_LANES, lane_tile, 0)

    y = y_scr[...] + cb_ref[...]
    mu = jnp.mean(y, axis=-1, keepdims=True)
    yc = y - mu
    var = jnp.mean(yc * yc, axis=-1, keepdims=True)
    yn = yc * lax.rsqrt(var + LN_EPS) * lg_ref[...] + lb_ref[...]
    o_ref[...] = (yn * _sigmoid(yn)).astype(o_ref.dtype)


def _conv_call(p, w, cb, lg, lb, *, batch, seq, ctx, col_a, col_b):
    n = p.shape[0]
    cw = w.shape[1]
    tc = 256
    assert seq % tc == 0 and ctx == tc
    hb = tc // CONV_HALO
    n_halo = n // CONV_HALO
    n_lat_blocks = batch * seq // tc

    def prev(i):
        return jnp.maximum(i * hb - 1, 0)

    def nxt(i):
        return jnp.minimum((i + 1) * hb, n_halo - 1)

    vec = lambda: pl.BlockSpec((1, cw), lambda i: (0, 0))
    return pl.pallas_call(
        functools.partial(_conv_kernel, tc=tc, blocks_per_seq=seq // tc, n_lat_blocks=n_lat_blocks),
        grid=(n // tc,),
        in_specs=[
            pl.BlockSpec((tc, cw), lambda i: (i, col_a)),
            pl.BlockSpec((tc, cw), lambda i: (i, col_b)),
            pl.BlockSpec((CONV_HALO, cw), lambda i: (prev(i), col_a)),
            pl.BlockSpec((CONV_HALO, cw), lambda i: (prev(i), col_b)),
            pl.BlockSpec((CONV_HALO, cw), lambda i: (nxt(i), col_a)),
            pl.BlockSpec((CONV_HALO, cw), lambda i: (nxt(i), col_b)),
            pl.BlockSpec((CONV_K, cw), lambda i: (0, 0)),
            vec(), vec(), vec(),
        ],
        out_specs=pl.BlockSpec((tc, cw), lambda i: (i, 0)),
        out_shape=jax.ShapeDtypeStruct((n, cw), BF16),
        scratch_shapes=[pltpu.VMEM((tc + 2 * CONV_HALO, cw), F32), pltpu.VMEM((tc, cw), F32)],
        compiler_params=_cparams(("parallel",)),
        name="conv_branch",
    )(p, p, p, p, p, p, w, cb.reshape(1, cw), lg.reshape(1, cw), lb.reshape(1, cw))


def _mixout_kernel(r_ref, u_ref, sr_ref, sc_ref, z_ref, mod_ref, wr_ref, wc_ref, wo_ref, o_ref):
    y_ret = jnp.dot(r_ref[...], wr_ref[...], preferred_element_type=F32)
    y_conv = jnp.dot(u_ref[...], wc_ref[...], preferred_element_type=F32)
    merged = (_sigmoid(sr_ref[...].astype(F32)) * y_ret
              + _sigmoid(sc_ref[...].astype(F32)) * y_conv)
    y = jnp.dot(merged.astype(BF16), wo_ref[...], preferred_element_type=F32)
    o_ref[...] = z_ref[...] + mod_ref[2:3, :] * y


def _mixout_call(r, u, p, z, mod, wr, wc, wo, *, mod_row, col_sr, col_sc):
    n, d = z.shape
    tm = 256
    const = lambda shape: pl.BlockSpec(shape, lambda i: (0, 0), pipeline_mode=pl.Buffered(1))
    return pl.pallas_call(
        _mixout_kernel,
        grid=(n // tm,),
        in_specs=[
            pl.BlockSpec((tm, r.shape[1]), lambda i: (i, 0)),
            pl.BlockSpec((tm, u.shape[1]), lambda i: (i, 0)),
            pl.BlockSpec((tm, d), lambda i: (i, col_sr)),
            pl.BlockSpec((tm, d), lambda i: (i, col_sc)),
            pl.BlockSpec((tm, d), lambda i: (i, 0)),
            pl.BlockSpec((None, N_MOD, d), lambda i: (mod_row(i * tm), 0, 0)),
            const(wr.shape), const(wc.shape), const(wo.shape),
        ],
        out_specs=pl.BlockSpec((tm, d), lambda i: (i, 0)),
        out_shape=jax.ShapeDtypeStruct((n, d), F32),
        compiler_params=_cparams(("parallel",)),
        name="mixer_out_proj",
    )(r, u, p, p, z, mod, wr, wc, wo)


def _router_kernel(z_ref, g_ref, mod_ref, rw_ref, rb_ref, tri_ref, h_ref, idx_ref, gate_ref, rank_ref, cnt_ref,
                   base_scr, *, tm, n_exp):
    @pl.when(pl.program_id(0) == 0)
    def _():
        base_scr[...] = jnp.zeros_like(base_scr)

    g = g_ref[...]
    shift = mod_ref[3:4, :]
    scale = mod_ref[4:5, :]
    rw = rw_ref[...]
    rb = rb_ref[...]
    tri = tri_ref[...]

    def body(r, carry):
        r0 = pl.multiple_of(r * ROW_SUB, ROW_SUB)
        rows = pl.ds(r0, ROW_SUB)
        h = _norm_mod(z_ref[rows, :], g, shift, scale).astype(BF16)
        h_ref[rows, :] = h
        lg = lax.dot_general(rw, h, (((1,), (1,)), ((), ())), preferred_element_type=F32) + rb
        eidx = lax.broadcasted_iota(jnp.int32, lg.shape, 0)
        vals, hits = [], []
        work = lg
        for _ in range(TOP_K):
            m = jnp.max(work, axis=0, keepdims=True)
            sel = jnp.min(jnp.where(work == m, eidx, n_exp), axis=0, keepdims=True)
            hit = eidx == sel
            vals.append(m)
            hits.append(hit)
            work = jnp.where(hit, -jnp.inf, work)
        ex = [jnp.exp(v - vals[0]) for v in vals]
        tot = ex[0] + ex[1] + ex[2] + ex[3]
        chosen = jnp.where(hits[0] | hits[1] | hits[2] | hits[3], 1.0, 0.0)
        before = jnp.dot(chosen.astype(BF16), tri, preferred_element_type=F32) + base_scr[:, 0:1]
        for s in range(TOP_K):
            idx_ref[s:s + 1, rows] = jnp.sum(jnp.where(hits[s], eidx, 0), axis=0, keepdims=True)
            gate_ref[s:s + 1, rows] = ex[s] / tot
            rank_ref[s:s + 1, rows] = jnp.sum(jnp.where(hits[s], before, 0.0), axis=0,
                                              keepdims=True).astype(jnp.int32)
        base_scr[...] += jnp.sum(chosen, axis=1, keepdims=True)
        return carry

    lax.fori_loop(0, tm // ROW_SUB, body, 0)
    cnt_ref[...] = base_scr[...].astype(jnp.int32)


def _router_call(z, n_rows, g, mod, rw_t, rb, *, mod_row, tm):
    d = z.shape[1]
    n_exp = rw_t.shape[0]
    tri = jnp.triu(jnp.ones((ROW_SUB, ROW_SUB), BF16), k=1)
    tok = lambda: pl.BlockSpec((TOP_K, tm), lambda i: (0, i))
    return pl.pallas_call(
        functools.partial(_router_kernel, tm=tm, n_exp=n_exp),
        grid=(n_rows // tm,),
        in_specs=[
            pl.BlockSpec((tm, d), lambda i: (i, 0)),
            pl.BlockSpec((1, d), lambda i: (0, 0)),
            pl.BlockSpec((None, N_MOD, d), lambda i: (mod_row(i * tm), 0, 0)),
            pl.BlockSpec((n_exp, d), lambda i: (0, 0)),
            pl.BlockSpec((n_exp, 1), lambda i: (0, 0)),
            pl.BlockSpec((ROW_SUB, ROW_SUB), lambda i: (0, 0)),
        ],
        out_specs=[
            pl.BlockSpec((tm, d), lambda i: (i, 0)),
            tok(), tok(), tok(),
            pl.BlockSpec((n_exp, 128), lambda i: (0, 0)),
        ],
        out_shape=[
            jax.ShapeDtypeStruct((n_rows, d), BF16),
            jax.ShapeDtypeStruct((TOP_K, n_rows), jnp.int32),
            jax.ShapeDtypeStruct((TOP_K, n_rows), F32),
            jax.ShapeDtypeStruct((TOP_K, n_rows), jnp.int32),
            jax.ShapeDtypeStruct((n_exp, 128), jnp.int32),
        ],
        scratch_shapes=[pltpu.VMEM((n_exp, 128), F32)],
        compiler_params=_cparams(("arbitrary",)),
        name="moe_router",
    )(z, g.reshape(1, d), mod, rw_t, rb.reshape(n_exp, 1), tri)


EXP_TILE = 512


def _cast_rows(src_ref, dst_ref):
    n = src_ref.shape[0]
    step = min(n, ROW_SUB)

    def body(r, carry):
        rows = pl.ds(pl.multiple_of(r * step, step), step)
        dst_ref[rows, :] = src_ref[rows, :].astype(dst_ref.dtype)
        return carry

    lax.fori_loop(0, n // step, body, 0)


def _experts_kernel(te_ref, nu_ref, x_ref, wg_ref, bg_ref, wu_ref, bu_ref, wd_ref, bd_ref, o_ref,
                    wg_s, wu_s, wd_s):
    j = pl.program_id(0)
    active = j < nu_ref[0]
    new_expert = jnp.logical_or(j == 0, te_ref[j] != te_ref[jnp.maximum(j - 1, 0)])

    @pl.when(jnp.logical_and(active, new_expert))
    def _():
        _cast_rows(wg_ref, wg_s)
        _cast_rows(wu_ref, wu_s)
        _cast_rows(wd_ref, wd_s)

    @pl.when(active)
    def _():
        x = x_ref[...]
        a = jnp.minimum(jnp.dot(x, wg_s[...], preferred_element_type=F32) + bg_ref[...], SWIGLU_LIMIT)
        b = jnp.clip(jnp.dot(x, wu_s[...], preferred_element_type=F32) + bu_ref[...],
                     -SWIGLU_LIMIT, SWIGLU_LIMIT)
        act = (b + 1.0) * a * _sigmoid(SWIGLU_ALPHA * a)
        y = jnp.dot(act.astype(BF16), wd_s[...], preferred_element_type=F32) + bd_ref[...]
        o_ref[...] = y.astype(o_ref.dtype)

    @pl.when(jnp.logical_not(active))
    def _():
        o_ref[...] = jnp.zeros_like(o_ref)


def _experts_call(tile_expert, n_used, xs, layer, wg, bg, wu, bu, wd, bd):
    rows, d = xs.shape
    depth, n_exp, _, de = wg.shape
    pick = lambda j, te, nu: (layer, te[j], 0, 0)
    grid_spec = pltpu.PrefetchScalarGridSpec(
        num_scalar_prefetch=2,
        grid=(rows // EXP_TILE,),
        in_specs=[
            pl.BlockSpec((EXP_TILE, d), lambda j, te, nu: (j, 0)),
            pl.BlockSpec((None, None, d, de), pick),
            pl.BlockSpec((None, None, 1, de), pick),
            pl.BlockSpec((None, None, d, de), pick),
            pl.BlockSpec((None, None, 1, de), pick),
            pl.BlockSpec((None, None, de, d), pick),
            pl.BlockSpec((None, None, 1, d), pick),
        ],
        out_specs=pl.BlockSpec((EXP_TILE, d), lambda j, te, nu: (j, 0)),
        scratch_shapes=[pltpu.VMEM((d, de), BF16), pltpu.VMEM((d, de), BF16), pltpu.VMEM((de, d), BF16)],
    )
    return pl.pallas_call(
        _experts_kernel,
        grid_spec=grid_spec,
        out_shape=jax.ShapeDtypeStruct((rows, d), BF16),
        compiler_params=_cparams(("arbitrary",)),
        name="moe_experts",
    )(tile_expert, n_used, xs, wg, bg.reshape(depth, n_exp, 1, de), wu, bu.reshape(depth, n_exp, 1, de),
      wd, bd.reshape(depth, n_exp, 1, d))


def _combine_kernel(y_ref, p_ref, z_ref, mod_ref, fg_ref, o_ref, *, final):
    p = p_ref[...]
    acc = p[:, 0:1] * y_ref[0].astype(F32)
    for s in range(1, TOP_K):
        acc += p[:, s:s + 1] * y_ref[s].astype(F32)
    z = z_ref[...] + mod_ref[5:6, :] * acc
    if final:
        ms = jnp.mean(z * z, axis=-1, keepdims=True)
        z = z * lax.rsqrt(ms + RMS_EPS) * fg_ref[...]
    o_ref[...] = z


def _combine_call(yg, gates_t, z, n_rows, mod, fg, *, mod_row, final):
    d = z.shape[1]
    tm = 256
    return pl.pallas_call(
        functools.partial(_combine_kernel, final=final),
        grid=(n_rows // tm,),
        in_specs=[
            pl.BlockSpec((TOP_K, tm, d), lambda i: (0, i, 0)),
            pl.BlockSpec((tm, TOP_K), lambda i: (i, 0)),
            pl.BlockSpec((tm, d), lambda i: (i, 0)),
            pl.BlockSpec((None, N_MOD, d), lambda i: (mod_row(i * tm), 0, 0)),
            pl.BlockSpec((1, d), lambda i: (0, 0)),
        ],
        out_specs=pl.BlockSpec((tm, d), lambda i: (i, 0)),
        out_shape=jax.ShapeDtypeStruct((n_rows, d), F32),
        compiler_params=_cparams(("parallel",)),
        name="moe_combine",
    )(yg, gates_t, z, mod, fg.reshape(1, d))


def _invert_kernel(pos_ref, init_ref, out_ref, *, n_tok, n_slots):
    del init_ref
    for s in range(n_slots):
        def body(t, carry, s=s):
            out_ref[pos_ref[s * n_tok + t]] = t
            return carry

        lax.fori_loop(0, n_tok, body, 0, unroll=8)


def _invert_call(pos, n_rows):
    k, n = pos.shape
    smem = pl.BlockSpec(memory_space=pltpu.SMEM)
    return pl.pallas_call(
        functools.partial(_invert_kernel, n_tok=n, n_slots=k),
        in_specs=[smem, smem],
        out_specs=smem,
        out_shape=jax.ShapeDtypeStruct((n_rows,), jnp.int32),
        input_output_aliases={1: 0},
        name="moe_row_tokens",
    )(pos.reshape(-1), jnp.zeros((n_rows,), jnp.int32))


def _dispatch_plan(idx, rank, counts):
    k, n = idx.shape
    n_exp = counts.shape[0]
    padded = ((counts + EXP_TILE - 1) // EXP_TILE) * EXP_TILE
    ends = jnp.cumsum(padded)
    offsets = ends - padded
    onehot = idx[:, :, None] == jnp.arange(n_exp, dtype=jnp.int32)[None, None, :]
    pos = rank + jnp.sum(jnp.where(onehot, offsets[None, None, :], 0), axis=-1)
    rows = k * n + n_exp * EXP_TILE
    row_token = _invert_call(pos, rows)
    n_tiles = rows // EXP_TILE
    tile_start = jnp.arange(n_tiles, dtype=jnp.int32) * EXP_TILE
    n_used = (ends[-1] // EXP_TILE).astype(jnp.int32)
    last_start = jnp.maximum(n_used - 1, 0) * EXP_TILE
    tile_expert = jnp.sum((ends[None, :] <= jnp.minimum(tile_start, last_start)[:, None]).astype(jnp.int32), axis=1)
    tile_expert = jnp.minimum(tile_expert, n_exp - 1)
    return pos, row_token, tile_expert, n_used.reshape(1)


def kernel(x, c, ctx, c_ctx, ada_w, ada_b, norm1_g, w_in, ret_decay_fwd, ret_decay_bwd, w_ret_o,
           conv_dw_w, conv_dw_b, conv_ln_g, conv_ln_b, w_conv_o, w_out, norm2_g, router_w, router_b,
           exp_w_gate, exp_b_gate, exp_w_up, exp_b_up, exp_w_down, exp_b_down, final_norm_g):
    batch, seq, d = x.shape
    n_ctx = ctx.shape[1]
    depth = ada_w.shape[0]
    n_exp = router_w.shape[2]
    assert batch + 1 <= MOD_ROWS and seq % CHUNK == 0 and n_ctx == CHUNK
    n_groups = 2 if batch % 2 == 0 else 1
    gb = batch // n_groups
    n_lat = gb * seq
    tm_big = _pick_tile(seq, 1024)
    while (gb * n_ctx) % tm_big:
        tm_big //= 2

    def make_mod_row(g):
        return lambda row0: jnp.where(row0 >= n_lat, batch, g * gb + row0 // seq)

    mod_rows = [make_mod_row(g) for g in range(n_groups)]
    cond = jnp.concatenate([c, c_ctx[None, :], jnp.zeros((MOD_ROWS - batch - 1, d), F32)], axis=0)
    mod_all = _ada_call(cond, ada_w, ada_b).reshape(depth, MOD_ROWS, N_MOD, d)
    cos_t, sin_t = _rope_tables(seq, n_ctx)
    cw = conv_dw_w.shape[2]
    col0 = 4 * RET_W // cw
    s_col = (4 * RET_W + 2 * cw) // d

    zs = [jnp.concatenate([x[g * gb:(g + 1) * gb].reshape(n_lat, d),
                           ctx[g * gb:(g + 1) * gb].reshape(gb * n_ctx, d)], axis=0) for g in range(n_groups)]

    for l in range(depth):
        mod = mod_all[l]
        last = l == depth - 1
        w_in_l = w_in[l].astype(BF16)
        w_ret_l = w_ret_o[l].astype(BF16)
        w_conv_l = w_conv_o[l].astype(BF16)
        w_out_l = w_out[l].astype(BF16)
        rw_t = router_w[l].T.astype(BF16)
        dmask, tab, dec = _retention_tables(ret_decay_fwd[l], ret_decay_bwd[l])
        staged = []
        for g in range(n_groups):
            z, mod_row = zs[g], mod_rows[g]
            p = _inproj_call(z, norm1_g[l], mod, w_in_l, mod_row=mod_row, tm=tm_big)
            r = _retention_call(p, dec, cos_t, sin_t, dmask, tab, batch=gb, seq=seq, ctx=n_ctx)
            u = _conv_call(p, conv_dw_w[l], conv_dw_b[l], conv_ln_g[l], conv_ln_b[l],
                           batch=gb, seq=seq, ctx=n_ctx, col_a=col0, col_b=col0 + 1)
            z = _mixout_call(r, u, p, z, mod, w_ret_l, w_conv_l, w_out_l, mod_row=mod_row,
                             col_sr=s_col, col_sc=s_col + 1)
            n_rows = n_lat if last else z.shape[0]
            h, idx, gates, rank, counts = _router_call(z, n_rows, norm2_g[l], mod, rw_t, router_b[l],
                                                       mod_row=mod_row, tm=tm_big)
            pos, row_token, tile_expert, n_used = _dispatch_plan(idx, rank, counts[:, 0])
            xs = h.at[row_token].get(mode="promise_in_bounds")
            staged.append((z, n_rows, gates, pos, tile_expert, n_used, xs))
        gathered = []
        for g in range(n_groups):
            z, n_rows, gates, pos, tile_expert, n_used, xs = staged[g]
            ys = _experts_call(tile_expert, n_used, xs, l, exp_w_gate, exp_b_gate, exp_w_up, exp_b_up,
                               exp_w_down, exp_b_down)
            gathered.append(ys.at[pos].get(mode="promise_in_bounds").reshape(TOP_K, n_rows, d))
        for g in range(n_groups):
            z, n_rows, gates = staged[g][:3]
            zs[g] = _combine_call(gathered[g], gates.T, z, n_rows, mod, final_norm_g, mod_row=mod_rows[g],
                                  final=last)
    return jnp.concatenate([z.reshape(gb, seq, d) for z in zs], axis=0)
```

```python
import functools

import jax
import jax.numpy as jnp
from jax import lax
from jax.experimental import pallas as pl
from jax.experimental.pallas import tpu as pltpu

F32 = jnp.float32
BF16 = jnp.bfloat16

GRID_W = 64
H_RET = 8
RET_D = 128
RET_W = H_RET * RET_D
ROPE_THETA = 10000.0
CONV_K = 31
CONV_HALO = 16
TOP_K = 4
SWIGLU_LIMIT = 7.0
SWIGLU_ALPHA = 1.702
N_MOD = 6
RMS_EPS = 1e-6
LN_EPS = 1e-5
MOD_ROWS = 16

CHUNK = 256
ROW_SUB = 256
VMEM_LIMIT = 56 * 1024 * 1024


def _cparams(sem):
    return pltpu.CompilerParams(dimension_semantics=sem, vmem_limit_bytes=VMEM_LIMIT)


def _pick_tile(n, pref):
    t = min(n, pref)
    while n % t:
        t //= 2
    return t


def _sigmoid(x):
    return 0.5 * jnp.tanh(0.5 * x) + 0.5


def _ada_kernel(cond_ref, w_ref, b_ref, o_ref):
    c = cond_ref[...]
    s = (c * _sigmoid(c)).astype(BF16)
    o_ref[...] = jnp.dot(s, w_ref[...].astype(BF16), preferred_element_type=F32) + b_ref[...]


def _ada_call(cond16, ada_w, ada_b):
    depth, d, nw = ada_w.shape
    tn = _pick_tile(nw, 1024)
    return pl.pallas_call(
        _ada_kernel,
        grid=(depth, nw // tn),
        in_specs=[
            pl.BlockSpec((MOD_ROWS, d), lambda l, j: (0, 0)),
            pl.BlockSpec((None, d, tn), lambda l, j: (l, 0, j)),
            pl.BlockSpec((None, 1, tn), lambda l, j: (l, 0, j)),
        ],
        out_specs=pl.BlockSpec((None, MOD_ROWS, tn), lambda l, j: (l, 0, j)),
        out_shape=jax.ShapeDtypeStruct((depth, MOD_ROWS, nw), F32),
        compiler_params=_cparams(("parallel", "parallel")),
        name="ada_mod",
    )(cond16, ada_w, ada_b.reshape(depth, 1, nw))


def _norm_mod(x, g, shift, scale):
    ms = jnp.mean(x * x, axis=-1, keepdims=True)
    y = x * lax.rsqrt(ms + RMS_EPS) * g
    return y * (1.0 + scale) + shift


def _inproj_kernel(z_ref, g_ref, mod_ref, w_ref, o_ref, h_scr, *, tm):
    @pl.when(pl.program_id(1) == 0)
    def _():
        g = g_ref[...]
        shift = mod_ref[0:1, :]
        scale = mod_ref[1:2, :]

        def body(r, carry):
            rows = pl.ds(pl.multiple_of(r * ROW_SUB, ROW_SUB), ROW_SUB)
            h_scr[rows, :] = _norm_mod(z_ref[rows, :], g, shift, scale).astype(BF16)
            return carry

        lax.fori_loop(0, tm // ROW_SUB, body, 0)

    o_ref[...] = jnp.dot(h_scr[...], w_ref[...], preferred_element_type=F32).astype(o_ref.dtype)


def _inproj_call(z, g, mod, w, *, mod_row, tm):
    n, d = z.shape
    nw = w.shape[1]
    tn = _pick_tile(nw, 1024)
    return pl.pallas_call(
        functools.partial(_inproj_kernel, tm=tm),
        grid=(n // tm, nw // tn),
        in_specs=[
            pl.BlockSpec((tm, d), lambda i, j: (i, 0)),
            pl.BlockSpec((1, d), lambda i, j: (0, 0)),
            pl.BlockSpec((None, N_MOD, d), lambda i, j: (mod_row(i * tm), 0, 0)),
            pl.BlockSpec((d, tn), lambda i, j: (0, j)),
        ],
        out_specs=pl.BlockSpec((tm, tn), lambda i, j: (i, j)),
        out_shape=jax.ShapeDtypeStruct((n, nw), BF16),
        scratch_shapes=[pltpu.VMEM((tm, d), BF16)],
        compiler_params=_cparams(("parallel", "arbitrary")),
        name="mixer_in_proj",
    )(z, g.reshape(1, d), mod, w)


def _rope(x, cos_t, sin_t):
    lane = lax.broadcasted_iota(jnp.int32, x.shape, 1)
    partner = jnp.where((lane % 64) < 32, pltpu.roll(x, 96, axis=1), pltpu.roll(x, 32, axis=1))
    return x * cos_t + partner * sin_t


def _ret_fwd_kernel(dec_ref, k_ref, v_ref, cos_ref, sin_ref, zf_ref, s_out_ref, s_scr):
    @pl.when(pl.program_id(1) == 0)
    def _():
        s_scr[...] = jnp.zeros_like(s_scr)

    cos_t = cos_ref[...]
    sin_t = sin_ref[...]
    for h in range(H_RET):
        cols = slice(h * RET_D, (h + 1) * RET_D)
        s_prev = s_scr[h]
        s_out_ref[h] = s_prev
        k = _rope(k_ref[:, cols].astype(F32), cos_t, sin_t) * (RET_D ** -0.5)
        kz = (k * zf_ref[h]).astype(BF16)
        ds = lax.dot_general(kz, v_ref[:, cols], (((0,), (0,)), ((), ())), preferred_element_type=F32)
        s_scr[h] = dec_ref[0, h] * s_prev + ds


def _ret_bwd_kernel(dec_ref, q_ref, k_ref, v_ref, g_ref, cos_ref, sin_ref, dm_ref, tab_ref, sf_ref,
                    o_ref, s_scr):
    @pl.when(pl.program_id(1) == 0)
    def _():
        s_scr[...] = jnp.zeros_like(s_scr)

    cos_t = cos_ref[...]
    sin_t = sin_ref[...]
    for h in range(H_RET):
        cols = slice(h * RET_D, (h + 1) * RET_D)
        q = _rope(q_ref[:, cols].astype(F32), cos_t, sin_t)
        k = _rope(k_ref[:, cols].astype(F32), cos_t, sin_t) * (RET_D ** -0.5)
        v = v_ref[:, cols]
        s = lax.dot_general(q.astype(BF16), k.astype(BF16), (((1,), (1,)), ((), ())),
                            preferred_element_type=F32)
        a = (s * dm_ref[h]).astype(BF16)
        sb = s_scr[h]
        o = jnp.dot(a, v, preferred_element_type=F32)
        o += jnp.dot((q * tab_ref[h, 0]).astype(BF16), sf_ref[h].astype(BF16), preferred_element_type=F32)
        o += jnp.dot((q * tab_ref[h, 1]).astype(BF16), sb.astype(BF16), preferred_element_type=F32)
        kz = (k * tab_ref[h, 2]).astype(BF16)
        ds = lax.dot_general(kz, v, (((0,), (0,)), ((), ())), preferred_element_type=F32)
        s_scr[h] = dec_ref[1, h] * sb + ds
        mu = jnp.mean(o, axis=-1, keepdims=True)
        oc = o - mu
        var = jnp.mean(oc * oc, axis=-1, keepdims=True)
        gate = g_ref[:, cols].astype(F32)
        o_ref[:, cols] = (oc * lax.rsqrt(var + LN_EPS) * (gate * _sigmoid(gate))).astype(o_ref.dtype)


def _retention_call(p, dec, cos_t, sin_t, dmask, tab, *, batch, seq, ctx):
    n = p.shape[0]
    nl = seq // CHUNK
    nlb = batch * nl
    steps = nl + 1
    assert ctx == CHUNK

    def fwd_rows(b, s):
        return jnp.where(s == 0, nlb + b, b * nl + s - 1)

    def fwd_pos(b, s):
        return jnp.where(s == 0, nl, s - 1)

    def bwd_rows(b, s):
        return jnp.where(s == 0, nlb + b, b * nl + nl - s)

    def bwd_pos(b, s):
        return jnp.where(s == 0, nl, nl - s)

    def bwd_state(b, s):
        return jnp.where(s == 0, 0, nl - s + 1)

    smem = pl.BlockSpec(memory_space=pltpu.SMEM)
    s_prev = pl.pallas_call(
        _ret_fwd_kernel,
        grid=(batch, steps),
        in_specs=[
            smem,
            pl.BlockSpec((CHUNK, RET_W), lambda b, s: (fwd_rows(b, s), 1)),
            pl.BlockSpec((CHUNK, RET_W), lambda b, s: (fwd_rows(b, s), 2)),
            pl.BlockSpec((CHUNK, RET_D), lambda b, s: (fwd_pos(b, s), 0)),
            pl.BlockSpec((CHUNK, RET_D), lambda b, s: (fwd_pos(b, s), 0)),
            pl.BlockSpec((H_RET, CHUNK, RET_D), lambda b, s: (0, 0, 0)),
        ],
        out_specs=pl.BlockSpec((None, None, H_RET, RET_D, RET_D), lambda b, s: (b, s, 0, 0, 0)),
        out_shape=jax.ShapeDtypeStruct((batch, steps, H_RET, RET_D, RET_D), F32),
        scratch_shapes=[pltpu.VMEM((H_RET, RET_D, RET_D), F32)],
        compiler_params=_cparams(("parallel", "arbitrary")),
        name="retention_fwd_state",
    )(dec, p, p, cos_t, sin_t, tab[:, 3])

    return pl.pallas_call(
        _ret_bwd_kernel,
        grid=(batch, steps),
        in_specs=[
            smem,
            pl.BlockSpec((CHUNK, RET_W), lambda b, s: (bwd_rows(b, s), 0)),
            pl.BlockSpec((CHUNK, RET_W), lambda b, s: (bwd_rows(b, s), 1)),
            pl.BlockSpec((CHUNK, RET_W), lambda b, s: (bwd_rows(b, s), 2)),
            pl.BlockSpec((CHUNK, RET_W), lambda b, s: (bwd_rows(b, s), 3)),
            pl.BlockSpec((CHUNK, RET_D), lambda b, s: (bwd_pos(b, s), 0)),
            pl.BlockSpec((CHUNK, RET_D), lambda b, s: (bwd_pos(b, s), 0)),
            pl.BlockSpec((H_RET, CHUNK, CHUNK), lambda b, s: (0, 0, 0)),
            pl.BlockSpec((H_RET, 3, CHUNK, RET_D), lambda b, s: (0, 0, 0, 0)),
            pl.BlockSpec((None, None, H_RET, RET_D, RET_D), lambda b, s: (b, bwd_state(b, s), 0, 0, 0)),
        ],
        out_specs=pl.BlockSpec((CHUNK, RET_W), lambda b, s: (bwd_rows(b, s), 0)),
        out_shape=jax.ShapeDtypeStruct((n, RET_W), BF16),
        scratch_shapes=[pltpu.VMEM((H_RET, RET_D, RET_D), F32)],
        compiler_params=_cparams(("parallel", "arbitrary")),
        name="retention_out",
    )(dec, p, p, p, p, cos_t, sin_t, dmask, tab[:, :3], s_prev)


def _retention_tables(dec_f, dec_b):
    lgf = jax.nn.log_sigmoid(dec_f.astype(F32))[:, None, None]
    lgb = jax.nn.log_sigmoid(dec_b.astype(F32))[:, None, None]
    i = jnp.arange(CHUNK, dtype=F32)
    diff = i[:, None] - i[None, :]
    dmask = jnp.where(diff >= 0, jnp.exp(lgf * jnp.maximum(diff, 0.0)), jnp.exp(lgb * jnp.maximum(-diff, 0.0)))
    col = i[None, :, None]
    ones = jnp.ones((1, 1, RET_D), F32)
    xi_f = jnp.exp(lgf * (col + 1.0)) * ones
    xi_b = jnp.exp(lgb * (CHUNK - col)) * ones
    zeta_b = jnp.exp(lgb * col) * ones
    zeta_f = jnp.exp(lgf * (CHUNK - 1.0 - col)) * ones
    tab = jnp.stack([xi_f, xi_b, zeta_b, zeta_f], axis=1)
    dec = jnp.stack([jnp.exp(lgf[:, 0, 0] * CHUNK), jnp.exp(lgb[:, 0, 0] * CHUNK)])
    return dmask, tab, dec


def _rope_tables(seq, ctx):
    quarter = RET_D // 4
    inv = ROPE_THETA ** (-jnp.arange(quarter, dtype=F32) / quarter)
    t = jnp.arange(seq)
    rows = (t // GRID_W).astype(F32)
    cols = (t % GRID_W).astype(F32)
    ar = rows[:, None] * inv[None, :]
    ac = cols[:, None] * inv[None, :]
    cos_t = jnp.concatenate([jnp.cos(ar), jnp.cos(ar), jnp.cos(ac), jnp.cos(ac)], axis=1)
    sin_t = jnp.concatenate([-jnp.sin(ar), jnp.sin(ar), -jnp.sin(ac), jnp.sin(ac)], axis=1)
    cos_t = jnp.concatenate([cos_t, jnp.ones((ctx, RET_D), F32)], axis=0)
    sin_t = jnp.concatenate([sin_t, jnp.zeros((ctx, RET_D), F32)], axis=0)
    return cos_t, sin_t


CONV_ROWS = 128
CONV_LANES = 128


def _glu(a, b):
    a = a.astype(F32)
    b = b.astype(F32)
    return a * _sigmoid(b)


def _conv_kernel(a_ref, b_ref, ap_ref, bp_ref, an_ref, bn_ref, w_ref, cb_ref, lg_ref, lb_ref,
                 o_ref, u_scr, y_scr, *, tc, blocks_per_seq, n_lat_blocks):
    i = pl.program_id(0)
    is_ctx = i >= n_lat_blocks
    first = jnp.logical_or(is_ctx, i % blocks_per_seq == 0)
    last = jnp.logical_or(is_ctx, i % blocks_per_seq == blocks_per_seq - 1)
    cw = a_ref.shape[1]
    u_scr[CONV_HALO:CONV_HALO + tc, :] = _glu(a_ref[...], b_ref[...])
    u_scr[0:CONV_HALO, :] = jnp.where(first, 0.0, _glu(ap_ref[...], bp_ref[...]))
    u_scr[CONV_HALO + tc:, :] = jnp.where(last, 0.0, _glu(an_ref[...], bn_ref[...]))

    base = CONV_HALO - CONV_K // 2

    def lane_tile(t, carry):
        lanes = pl.ds(pl.multiple_of(t * CONV_LANES, CONV_LANES), CONV_LANES)
        for r0 in range(0, tc, CONV_ROWS):
            strip = u_scr[r0:r0 + CONV_ROWS + 2 * CONV_HALO, lanes]
            acc = jnp.zeros((CONV_ROWS, CONV_LANES), F32)
            for phase in range(8):
                shifted = strip if phase == 0 else pltpu.roll(strip, strip.shape[0] - phase, axis=0)
                for kk in range(CONV_K):
                    if (base + kk) % 8 == phase:
                        lo = base + kk - phase
                        acc = acc + shifted[lo:lo + CONV_ROWS] * w_ref[kk:kk + 1, lanes]
            y_scr[r0:r0 + CONV_ROWS, lanes] = acc
        return carry

    lax.fori_loop(0, cw // CONV_LANES, lane_tile, 0)

    y = y_scr[...] + cb_ref[...]
    mu = jnp.mean(y, axis=-1, keepdims=True)
    yc = y - mu
    var = jnp.mean(yc * yc, axis=-1, keepdims=True)
    yn = yc * lax.rsqrt(var + LN_EPS) * lg_ref[...] + lb_ref[...]
    o_ref[...] = (yn * _sigmoid(yn)).astype(o_ref.dtype)


def _conv_call(p, w, cb, lg, lb, *, batch, seq, ctx, col_a, col_b):
    n = p.shape[0]
    cw = w.shape[1]
    tc = 256
    assert seq % tc == 0 and ctx == tc
    hb = tc // CONV_HALO
    n_halo = n // CONV_HALO
    n_lat_blocks = batch * seq // tc

    def prev(i):
        return jnp.maximum(i * hb - 1, 0)

    def nxt(i):
        return jnp.minimum((i + 1) * hb, n_halo - 1)

    vec = lambda: pl.BlockSpec((1, cw), lambda i: (0, 0))
    return pl.pallas_call(
        functools.partial(_conv_kernel, tc=tc, blocks_per_seq=seq // tc, n_lat_blocks=n_lat_blocks),
        grid=(n // tc,),
        in_specs=[
            pl.BlockSpec((tc, cw), lambda i: (i, col_a)),
            pl.BlockSpec((tc, cw), lambda i: (i, col_b)),
            pl.BlockSpec((CONV_HALO, cw), lambda i: (prev(i), col_a)),
            pl.BlockSpec((CONV_HALO, cw), lambda i: (prev(i), col_b)),
            pl.BlockSpec((CONV_HALO, cw), lambda i: (nxt(i), col_a)),
            pl.BlockSpec((CONV_HALO, cw), lambda i: (nxt(i), col_b)),
            pl.BlockSpec((CONV_K, cw), lambda i: (0, 0)),
            vec(), vec(), vec(),
        ],
        out_specs=pl.BlockSpec((tc, cw), lambda i: (i, 0)),
        out_shape=jax.ShapeDtypeStruct((n, cw), BF16),
        scratch_shapes=[pltpu.VMEM((tc + 2 * CONV_HALO, cw), F32), pltpu.VMEM((tc, cw), F32)],
        compiler_params=_cparams(("parallel",)),
        name="conv_branch",
    )(p, p, p, p, p, p, w, cb.reshape(1, cw), lg.reshape(1, cw), lb.reshape(1, cw))


def _mixout_kernel(r_ref, u_ref, sr_ref, sc_ref, z_ref, mod_ref, wr_ref, wc_ref, wo_ref, o_ref):
    y_ret = jnp.dot(r_ref[...], wr_ref[...], preferred_element_type=F32)
    y_conv = jnp.dot(u_ref[...], wc_ref[...], preferred_element_type=F32)
    merged = (_sigmoid(sr_ref[...].astype(F32)) * y_ret
              + _sigmoid(sc_ref[...].astype(F32)) * y_conv)
    y = jnp.dot(merged.astype(BF16), wo_ref[...], preferred_element_type=F32)
    o_ref[...] = z_ref[...] + mod_ref[2:3, :] * y


def _mixout_call(r, u, p, z, mod, wr, wc, wo, *, mod_row, col_sr, col_sc):
    n, d = z.shape
    tm = 256
    const = lambda shape: pl.BlockSpec(shape, lambda i: (0, 0), pipeline_mode=pl.Buffered(1))
    return pl.pallas_call(
        _mixout_kernel,
        grid=(n // tm,),
        in_specs=[
            pl.BlockSpec((tm, r.shape[1]), lambda i: (i, 0)),
            pl.BlockSpec((tm, u.shape[1]), lambda i: (i, 0)),
            pl.BlockSpec((tm, d), lambda i: (i, col_sr)),
            pl.BlockSpec((tm, d), lambda i: (i, col_sc)),
            pl.BlockSpec((tm, d), lambda i: (i, 0)),
            pl.BlockSpec((None, N_MOD, d), lambda i: (mod_row(i * tm), 0, 0)),
            const(wr.shape), const(wc.shape), const(wo.shape),
        ],
        out_specs=pl.BlockSpec((tm, d), lambda i: (i, 0)),
        out_shape=jax.ShapeDtypeStruct((n, d), F32),
        compiler_params=_cparams(("parallel",)),
        name="mixer_out_proj",
    )(r, u, p, p, z, mod, wr, wc, wo)


def _router_kernel(z_ref, g_ref, mod_ref, rw_ref, rb_ref, tri_ref, h_ref, idx_ref, gate_ref, rank_ref, cnt_ref,
                   base_scr, *, tm, n_exp):
    @pl.when(pl.program_id(0) == 0)
    def _():
        base_scr[...] = jnp.zeros_like(base_scr)

    g = g_ref[...]
    shift = mod_ref[3:4, :]
    scale = mod_ref[4:5, :]
    rw = rw_ref[...]
    rb = rb_ref[...]
    tri = tri_ref[...]

    def body(r, carry):
        r0 = pl.multiple_of(r * ROW_SUB, ROW_SUB)
        rows = pl.ds(r0, ROW_SUB)
        h = _norm_mod(z_ref[rows, :], g, shift, scale).astype(BF16)
        h_ref[rows, :] = h
        lg = lax.dot_general(rw, h, (((1,), (1,)), ((), ())), preferred_element_type=F32) + rb
        eidx = lax.broadcasted_iota(jnp.int32, lg.shape, 0)
        vals, hits = [], []
        work = lg
        for _ in range(TOP_K):
            m = jnp.max(work, axis=0, keepdims=True)
            sel = jnp.min(jnp.where(work == m, eidx, n_exp), axis=0, keepdims=True)
            hit = eidx == sel
            vals.append(m)
            hits.append(hit)
            work = jnp.where(hit, -jnp.inf, work)
        ex = [jnp.exp(v - vals[0]) for v in vals]
        tot = ex[0] + ex[1] + ex[2] + ex[3]
        chosen = jnp.where(hits[0] | hits[1] | hits[2] | hits[3], 1.0, 0.0)
        before = jnp.dot(chosen.astype(BF16), tri, preferred_element_type=F32) + base_scr[:, 0:1]
        for s in range(TOP_K):
            idx_ref[s:s + 1, rows] = jnp.sum(jnp.where(hits[s], eidx, 0), axis=0, keepdims=True)
            gate_ref[s:s + 1, rows] = ex[s] / tot
            rank_ref[s:s + 1, rows] = jnp.sum(jnp.where(hits[s], before, 0.0), axis=0,
                                              keepdims=True).astype(jnp.int32)
        base_scr[...] += jnp.sum(chosen, axis=1, keepdims=True)
        return carry

    lax.fori_loop(0, tm // ROW_SUB, body, 0)
    cnt_ref[...] = base_scr[...].astype(jnp.int32)


def _router_call(z, n_rows, g, mod, rw_t, rb, *, mod_row, tm):
    d = z.shape[1]
    n_exp = rw_t.shape[0]
    tri = jnp.triu(jnp.ones((ROW_SUB, ROW_SUB), BF16), k=1)
    tok = lambda: pl.BlockSpec((TOP_K, tm), lambda i: (0, i))
    return pl.pallas_call(
        functools.partial(_router_kernel, tm=tm, n_exp=n_exp),
        grid=(n_rows // tm,),
        in_specs=[
            pl.BlockSpec((tm, d), lambda i: (i, 0)),
            pl.BlockSpec((1, d), lambda i: (0, 0)),
            pl.BlockSpec((None, N_MOD, d), lambda i: (mod_row(i * tm), 0, 0)),
            pl.BlockSpec((n_exp, d), lambda i: (0, 0)),
            pl.BlockSpec((n_exp, 1), lambda i: (0, 0)),
            pl.BlockSpec((ROW_SUB, ROW_SUB), lambda i: (0, 0)),
        ],
        out_specs=[
            pl.BlockSpec((tm, d), lambda i: (i, 0)),
            tok(), tok(), tok(),
            pl.BlockSpec((n_exp, 128), lambda i: (0, 0)),
        ],
        out_shape=[
            jax.ShapeDtypeStruct((n_rows, d), BF16),
            jax.ShapeDtypeStruct((TOP_K, n_rows), jnp.int32),
            jax.ShapeDtypeStruct((TOP_K, n_rows), F32),
            jax.ShapeDtypeStruct((TOP_K, n_rows), jnp.int32),
            jax.ShapeDtypeStruct((n_exp, 128), jnp.int32),
        ],
        scratch_shapes=[pltpu.VMEM((n_exp, 128), F32)],
        compiler_params=_cparams(("arbitrary",)),
        name="moe_router",
    )(z, g.reshape(1, d), mod, rw_t, rb.reshape(n_exp, 1), tri)


EXP_TILE = 512


def _cast_rows(src_ref, dst_ref):
    n = src_ref.shape[0]
    step = min(n, ROW_SUB)

    def body(r, carry):
        rows = pl.ds(pl.multiple_of(r * step, step), step)
        dst_ref[rows, :] = src_ref[rows, :].astype(dst_ref.dtype)
        return carry

    lax.fori_loop(0, n // step, body, 0)


def _experts_kernel(te_ref, nu_ref, nxt_ref, x_ref, wg_hbm, bg_ref, wu_hbm, bu_ref, wd_hbm, bd_ref, o_ref,
                    wg_f, wu_f, wd_f, wg_s, wu_s, wd_s, sems, *, layer):
    j = pl.program_id(0)
    active = j < nu_ref[0]
    expert = te_ref[j]
    new_expert = jnp.logical_or(j == 0, expert != te_ref[jnp.maximum(j - 1, 0)])

    def weight_copies(e):
        return (pltpu.make_async_copy(wg_hbm.at[layer, e], wg_f, sems.at[0]),
                pltpu.make_async_copy(wu_hbm.at[layer, e], wu_f, sems.at[1]),
                pltpu.make_async_copy(wd_hbm.at[layer, e], wd_f, sems.at[2]))

    @pl.when(jnp.logical_and(active, j == 0))
    def _():
        for cp in weight_copies(expert):
            cp.start()

    @pl.when(jnp.logical_and(active, new_expert))
    def _():
        for cp in weight_copies(expert):
            cp.wait()
        _cast_rows(wg_f, wg_s)
        _cast_rows(wu_f, wu_s)
        _cast_rows(wd_f, wd_s)
        nxt = nxt_ref[expert]

        @pl.when(nxt >= 0)
        def _():
            for cp in weight_copies(nxt):
                cp.start()

    @pl.when(active)
    def _():
        x = x_ref[...]
        a = jnp.minimum(jnp.dot(x, wg_s[...], preferred_element_type=F32) + bg_ref[...], SWIGLU_LIMIT)
        b = jnp.clip(jnp.dot(x, wu_s[...], preferred_element_type=F32) + bu_ref[...],
                     -SWIGLU_LIMIT, SWIGLU_LIMIT)
        act = (b + 1.0) * a * _sigmoid(SWIGLU_ALPHA * a)
        y = jnp.dot(act.astype(BF16), wd_s[...], preferred_element_type=F32) + bd_ref[...]
        o_ref[...] = y.astype(o_ref.dtype)

    @pl.when(jnp.logical_not(active))
    def _():
        o_ref[...] = jnp.zeros_like(o_ref)


def _experts_call(tile_expert, n_used, next_expert, xs, layer, wg, bg, wu, bu, wd, bd):
    rows, d = xs.shape
    depth, n_exp, _, de = wg.shape
    pick = lambda j, te, nu, nxt: (layer, te[j], 0, 0)
    hbm = pl.BlockSpec(memory_space=pl.ANY)
    grid_spec = pltpu.PrefetchScalarGridSpec(
        num_scalar_prefetch=3,
        grid=(rows // EXP_TILE,),
        in_specs=[
            pl.BlockSpec((EXP_TILE, d), lambda j, te, nu, nxt: (j, 0)),
            hbm,
            pl.BlockSpec((None, None, 1, de), pick),
            hbm,
            pl.BlockSpec((None, None, 1, de), pick),
            hbm,
            pl.BlockSpec((None, None, 1, d), pick),
        ],
        out_specs=pl.BlockSpec((EXP_TILE, d), lambda j, te, nu, nxt: (j, 0)),
        scratch_shapes=[pltpu.VMEM((d, de), F32), pltpu.VMEM((d, de), F32), pltpu.VMEM((de, d), F32),
                        pltpu.VMEM((d, de), BF16), pltpu.VMEM((d, de), BF16), pltpu.VMEM((de, d), BF16),
                        pltpu.SemaphoreType.DMA((3,))],
    )
    return pl.pallas_call(
        functools.partial(_experts_kernel, layer=layer),
        grid_spec=grid_spec,
        out_shape=jax.ShapeDtypeStruct((rows, d), BF16),
        compiler_params=_cparams(("arbitrary",)),
        name="moe_experts",
    )(tile_expert, n_used, next_expert, xs, wg, bg.reshape(depth, n_exp, 1, de), wu,
      bu.reshape(depth, n_exp, 1, de), wd, bd.reshape(depth, n_exp, 1, d))


def _combine_kernel(y_ref, p_ref, z_ref, mod_ref, fg_ref, o_ref, *, final):
    p = p_ref[...]
    acc = p[:, 0:1] * y_ref[0].astype(F32)
    for s in range(1, TOP_K):
        acc += p[:, s:s + 1] * y_ref[s].astype(F32)
    z = z_ref[...] + mod_ref[5:6, :] * acc
    if final:
        ms = jnp.mean(z * z, axis=-1, keepdims=True)
        z = z * lax.rsqrt(ms + RMS_EPS) * fg_ref[...]
    o_ref[...] = z


def _combine_call(yg, gates_t, z, n_rows, mod, fg, *, mod_row, final):
    d = z.shape[1]
    tm = 256
    return pl.pallas_call(
        functools.partial(_combine_kernel, final=final),
        grid=(n_rows // tm,),
        in_specs=[
            pl.BlockSpec((TOP_K, tm, d), lambda i: (0, i, 0)),
            pl.BlockSpec((tm, TOP_K), lambda i: (i, 0)),
            pl.BlockSpec((tm, d), lambda i: (i, 0)),
            pl.BlockSpec((None, N_MOD, d), lambda i: (mod_row(i * tm), 0, 0)),
            pl.BlockSpec((1, d), lambda i: (0, 0)),
        ],
        out_specs=pl.BlockSpec((tm, d), lambda i: (i, 0)),
        out_shape=jax.ShapeDtypeStruct((n_rows, d), F32),
        compiler_params=_cparams(("parallel",)),
        name="moe_combine",
    )(yg, gates_t, z, mod, fg.reshape(1, d))


def _invert_kernel(pos_ref, init_hbm, out_ref, *, n_tok, n_slots):
    pltpu.sync_copy(init_hbm, out_ref)
    for s in range(n_slots):
        def body(t, carry, s=s):
            out_ref[pos_ref[s * n_tok + t]] = t
            return carry

        lax.fori_loop(0, n_tok, body, 0, unroll=8)


def _invert_call(pos, n_rows):
    k, n = pos.shape
    smem = pl.BlockSpec(memory_space=pltpu.SMEM)
    return pl.pallas_call(
        functools.partial(_invert_kernel, n_tok=n, n_slots=k),
        in_specs=[smem, pl.BlockSpec(memory_space=pl.ANY)],
        out_specs=smem,
        out_shape=jax.ShapeDtypeStruct((n_rows,), jnp.int32),
        name="moe_row_tokens",
    )(pos.reshape(-1), jnp.arange(n_rows, dtype=jnp.int32) % n)


def _dispatch_plan(idx, rank, counts):
    k, n = idx.shape
    n_exp = counts.shape[0]
    padded = ((counts + EXP_TILE - 1) // EXP_TILE) * EXP_TILE
    ends = jnp.cumsum(padded)
    offsets = ends - padded
    onehot = idx[:, :, None] == jnp.arange(n_exp, dtype=jnp.int32)[None, None, :]
    pos = rank + jnp.sum(jnp.where(onehot, offsets[None, None, :], 0), axis=-1)
    rows = k * n + n_exp * EXP_TILE
    row_token = _invert_call(pos, rows)
    n_tiles = rows // EXP_TILE
    tile_start = jnp.arange(n_tiles, dtype=jnp.int32) * EXP_TILE
    n_used = (ends[-1] // EXP_TILE).astype(jnp.int32)
    last_start = jnp.maximum(n_used - 1, 0) * EXP_TILE
    tile_expert = jnp.sum((ends[None, :] <= jnp.minimum(tile_start, last_start)[:, None]).astype(jnp.int32), axis=1)
    tile_expert = jnp.minimum(tile_expert, n_exp - 1)
    e_ids = jnp.arange(n_exp, dtype=jnp.int32)
    later_used = jnp.logical_and(e_ids[None, :] > e_ids[:, None], (counts > 0)[None, :])
    next_expert = jnp.min(jnp.where(later_used, e_ids[None, :], n_exp), axis=1)
    next_expert = jnp.where(next_expert == n_exp, -1, next_expert).astype(jnp.int32)
    return pos, row_token, tile_expert, n_used.reshape(1), next_expert


def kernel(x, c, ctx, c_ctx, ada_w, ada_b, norm1_g, w_in, ret_decay_fwd, ret_decay_bwd, w_ret_o,
           conv_dw_w, conv_dw_b, conv_ln_g, conv_ln_b, w_conv_o, w_out, norm2_g, router_w, router_b,
           exp_w_gate, exp_b_gate, exp_w_up, exp_b_up, exp_w_down, exp_b_down, final_norm_g):
    batch, seq, d = x.shape
    n_ctx = ctx.shape[1]
    depth = ada_w.shape[0]
    n_exp = router_w.shape[2]
    assert batch + 1 <= MOD_ROWS and seq % CHUNK == 0 and n_ctx == CHUNK
    n_groups = 2 if batch % 2 == 0 else 1
    gb = batch // n_groups
    n_lat = gb * seq
    tm_big = _pick_tile(seq, 1024)
    while (gb * n_ctx) % tm_big:
        tm_big //= 2

    def make_mod_row(g):
        return lambda row0: jnp.where(row0 >= n_lat, batch, g * gb + row0 // seq)

    mod_rows = [make_mod_row(g) for g in range(n_groups)]
    cond = jnp.concatenate([c, c_ctx[None, :], jnp.zeros((MOD_ROWS - batch - 1, d), F32)], axis=0)
    mod_all = _ada_call(cond, ada_w, ada_b).reshape(depth, MOD_ROWS, N_MOD, d)
    cos_t, sin_t = _rope_tables(seq, n_ctx)
    cw = conv_dw_w.shape[2]
    col0 = 4 * RET_W // cw
    s_col = (4 * RET_W + 2 * cw) // d

    zs = [jnp.concatenate([x[g * gb:(g + 1) * gb].reshape(n_lat, d),
                           ctx[g * gb:(g + 1) * gb].reshape(gb * n_ctx, d)], axis=0) for g in range(n_groups)]

    for l in range(depth):
        mod = mod_all[l]
        last = l == depth - 1
        w_in_l = w_in[l].astype(BF16)
        w_ret_l = w_ret_o[l].astype(BF16)
        w_conv_l = w_conv_o[l].astype(BF16)
        w_out_l = w_out[l].astype(BF16)
        rw_t = router_w[l].T.astype(BF16)
        dmask, tab, dec = _retention_tables(ret_decay_fwd[l], ret_decay_bwd[l])
        staged = []
        for g in range(n_groups):
            z, mod_row = zs[g], mod_rows[g]
            p = _inproj_call(z, norm1_g[l], mod, w_in_l, mod_row=mod_row, tm=tm_big)
            r = _retention_call(p, dec, cos_t, sin_t, dmask, tab, batch=gb, seq=seq, ctx=n_ctx)
            u = _conv_call(p, conv_dw_w[l], conv_dw_b[l], conv_ln_g[l], conv_ln_b[l],
                           batch=gb, seq=seq, ctx=n_ctx, col_a=col0, col_b=col0 + 1)
            z = _mixout_call(r, u, p, z, mod, w_ret_l, w_conv_l, w_out_l, mod_row=mod_row,
                             col_sr=s_col, col_sc=s_col + 1)
            n_rows = n_lat if last else z.shape[0]
            h, idx, gates, rank, counts = _router_call(z, n_rows, norm2_g[l], mod, rw_t, router_b[l],
                                                       mod_row=mod_row, tm=tm_big)
            pos, row_token, tile_expert, n_used, next_expert = _dispatch_plan(idx, rank, counts[:, 0])
            xs = h.at[row_token].get(mode="promise_in_bounds")
            staged.append((z, n_rows, gates, pos, tile_expert, n_used, next_expert, xs))
        gathered = []
        for g in range(n_groups):
            z, n_rows, gates, pos, tile_expert, n_used, next_expert, xs = staged[g]
            ys = _experts_call(tile_expert, n_used, next_expert, xs, l, exp_w_gate, exp_b_gate, exp_w_up, exp_b_up,
                               exp_w_down, exp_b_down)
            gathered.append(ys.at[pos].get(mode="promise_in_bounds").reshape(TOP_K, n_rows, d))
        for g in range(n_groups):
            z, n_rows, gates = staged[g][:3]
            zs[g] = _combine_call(gathered[g], gates.T, z, n_rows, mod, final_norm_g, mod_row=mod_rows[g],
                                  final=last)
    return jnp.concatenate([z.reshape(gb, seq, d) for z in zs], axis=0)
```

```python
import functools

import jax
import jax.numpy as jnp
from jax import lax
from jax.experimental import pallas as pl
from jax.experimental.pallas import tpu as pltpu

F32 = jnp.float32
BF16 = jnp.bfloat16

GRID_W = 64
H_RET = 8
RET_D = 128
RET_W = H_RET * RET_D
ROPE_THETA = 10000.0
CONV_K = 31
CONV_HALO = 16
TOP_K = 4
SWIGLU_LIMIT = 7.0
SWIGLU_ALPHA = 1.702
N_MOD = 6
RMS_EPS = 1e-6
LN_EPS = 1e-5
MOD_ROWS = 16

CHUNK = 256
ROW_SUB = 256
VMEM_LIMIT = 56 * 1024 * 1024


def _cparams(sem):
    return pltpu.CompilerParams(dimension_semantics=sem, vmem_limit_bytes=VMEM_LIMIT)


def _pick_tile(n, pref):
    t = min(n, pref)
    while n % t:
        t //= 2
    return t


def _after(x, dep):
    x, _ = lax.optimization_barrier((x, dep))
    return x


def _sigmoid(x):
    return 0.5 * jnp.tanh(0.5 * x) + 0.5


def _ada_kernel(cond_ref, w_ref, b_ref, o_ref):
    c = cond_ref[...]
    s = (c * _sigmoid(c)).astype(BF16)
    o_ref[...] = jnp.dot(s, w_ref[...].astype(BF16), preferred_element_type=F32) + b_ref[...]


def _ada_call(cond16, ada_w, ada_b):
    depth, d, nw = ada_w.shape
    tn = _pick_tile(nw, 1024)
    return pl.pallas_call(
        _ada_kernel,
        grid=(depth, nw // tn),
        in_specs=[
            pl.BlockSpec((MOD_ROWS, d), lambda l, j: (0, 0)),
            pl.BlockSpec((None, d, tn), lambda l, j: (l, 0, j)),
            pl.BlockSpec((None, 1, tn), lambda l, j: (l, 0, j)),
        ],
        out_specs=pl.BlockSpec((None, MOD_ROWS, tn), lambda l, j: (l, 0, j)),
        out_shape=jax.ShapeDtypeStruct((depth, MOD_ROWS, nw), F32),
        compiler_params=_cparams(("parallel", "parallel")),
        name="ada_mod",
    )(cond16, ada_w, ada_b.reshape(depth, 1, nw))


def _norm_mod(x, g, shift, scale):
    ms = jnp.mean(x * x, axis=-1, keepdims=True)
    y = x * lax.rsqrt(ms + RMS_EPS) * g
    return y * (1.0 + scale) + shift


def _inproj_kernel(z_ref, g_ref, mod_ref, w_ref, o_ref, h_scr, *, tm):
    @pl.when(pl.program_id(1) == 0)
    def _():
        g = g_ref[...]
        shift = mod_ref[0:1, :]
        scale = mod_ref[1:2, :]

        def body(r, carry):
            rows = pl.ds(pl.multiple_of(r * ROW_SUB, ROW_SUB), ROW_SUB)
            h_scr[rows, :] = _norm_mod(z_ref[rows, :], g, shift, scale).astype(BF16)
            return carry

        lax.fori_loop(0, tm // ROW_SUB, body, 0)

    o_ref[...] = jnp.dot(h_scr[...], w_ref[...], preferred_element_type=F32).astype(o_ref.dtype)


def _inproj_call(z, g, mod, w, *, mod_row, tm):
    n, d = z.shape
    nw = w.shape[1]
    tn = _pick_tile(nw, 1024)
    return pl.pallas_call(
        functools.partial(_inproj_kernel, tm=tm),
        grid=(n // tm, nw // tn),
        in_specs=[
            pl.BlockSpec((tm, d), lambda i, j: (i, 0)),
            pl.BlockSpec((1, d), lambda i, j: (0, 0)),
            pl.BlockSpec((None, N_MOD, d), lambda i, j: (mod_row(i * tm), 0, 0)),
            pl.BlockSpec((d, tn), lambda i, j: (0, j)),
        ],
        out_specs=pl.BlockSpec((tm, tn), lambda i, j: (i, j)),
        out_shape=jax.ShapeDtypeStruct((n, nw), BF16),
        scratch_shapes=[pltpu.VMEM((tm, d), BF16)],
        compiler_params=_cparams(("parallel", "arbitrary")),
        name="mixer_in_proj",
    )(z, g.reshape(1, d), mod, w)


def _rope(x, cos_t, sin_t):
    lane = lax.broadcasted_iota(jnp.int32, x.shape, 1)
    partner = jnp.where((lane % 64) < 32, pltpu.roll(x, 96, axis=1), pltpu.roll(x, 32, axis=1))
    return x * cos_t + partner * sin_t


def _ret_fwd_kernel(dec_ref, k_ref, v_ref, cos_ref, sin_ref, zf_ref, s_out_ref, s_scr):
    @pl.when(pl.program_id(1) == 0)
    def _():
        s_scr[...] = jnp.zeros_like(s_scr)

    cos_t = cos_ref[...]
    sin_t = sin_ref[...]
    for h in range(H_RET):
        cols = slice(h * RET_D, (h + 1) * RET_D)
        s_prev = s_scr[h]
        s_out_ref[h] = s_prev
        k = _rope(k_ref[:, cols].astype(F32), cos_t, sin_t)
        kz = (k * zf_ref[h]).astype(BF16)
        ds = lax.dot_general(kz, v_ref[:, cols], (((0,), (0,)), ((), ())), preferred_element_type=F32)
        s_scr[h] = dec_ref[0, h] * s_prev + ds


def _ret_bwd_kernel(dec_ref, q_ref, k_ref, v_ref, g_ref, cos_ref, sin_ref, dm_ref, tab_ref, sf_ref,
                    o_ref, s_scr):
    @pl.when(pl.program_id(1) == 0)
    def _():
        s_scr[...] = jnp.zeros_like(s_scr)

    cos_t = cos_ref[...]
    sin_t = sin_ref[...]
    for h in range(H_RET):
        cols = slice(h * RET_D, (h + 1) * RET_D)
        q = _rope(q_ref[:, cols].astype(F32), cos_t, sin_t)
        k = _rope(k_ref[:, cols].astype(F32), cos_t, sin_t)
        v = v_ref[:, cols]
        s = lax.dot_general(q.astype(BF16), k.astype(BF16), (((1,), (1,)), ((), ())),
                            preferred_element_type=F32)
        a = (s * dm_ref[h]).astype(BF16)
        sb = s_scr[h]
        o = jnp.dot(a, v, preferred_element_type=F32)
        o += jnp.dot((q * tab_ref[h, 0]).astype(BF16), sf_ref[h].astype(BF16), preferred_element_type=F32)
        o += jnp.dot((q * tab_ref[h, 1]).astype(BF16), sb.astype(BF16), preferred_element_type=F32)
        kz = (k * tab_ref[h, 2]).astype(BF16)
        ds = lax.dot_general(kz, v, (((0,), (0,)), ((), ())), preferred_element_type=F32)
        s_scr[h] = dec_ref[1, h] * sb + ds
        mu = jnp.mean(o, axis=-1, keepdims=True)
        oc = o - mu
        var = jnp.mean(oc * oc, axis=-1, keepdims=True)
        gate = g_ref[:, cols].astype(F32)
        o_ref[:, cols] = (oc * lax.rsqrt(var + LN_EPS) * (gate * _sigmoid(gate))).astype(o_ref.dtype)


def _retention_call(p, dec, cos_t, sin_t, dmask, tab, *, batch, seq, ctx):
    n = p.shape[0]
    nl = seq // CHUNK
    nlb = batch * nl
    steps = nl + 1
    assert ctx == CHUNK

    def fwd_rows(b, s):
        return jnp.where(s == 0, nlb + b, b * nl + s - 1)

    def fwd_pos(b, s):
        return jnp.where(s == 0, nl, s - 1)

    def bwd_rows(b, s):
        return jnp.where(s == 0, nlb + b, b * nl + nl - s)

    def bwd_pos(b, s):
        return jnp.where(s == 0, nl, nl - s)

    def bwd_state(b, s):
        return jnp.where(s == 0, 0, nl - s + 1)

    smem = pl.BlockSpec(memory_space=pltpu.SMEM)
    s_prev = pl.pallas_call(
        _ret_fwd_kernel,
        grid=(batch, steps),
        in_specs=[
            smem,
            pl.BlockSpec((CHUNK, RET_W), lambda b, s: (fwd_rows(b, s), 1)),
            pl.BlockSpec((CHUNK, RET_W), lambda b, s: (fwd_rows(b, s), 2)),
            pl.BlockSpec((CHUNK, RET_D), lambda b, s: (fwd_pos(b, s), 0)),
            pl.BlockSpec((CHUNK, RET_D), lambda b, s: (fwd_pos(b, s), 0)),
            pl.BlockSpec((H_RET, CHUNK, RET_D), lambda b, s: (0, 0, 0)),
        ],
        out_specs=pl.BlockSpec((None, None, H_RET, RET_D, RET_D), lambda b, s: (b, s, 0, 0, 0)),
        out_shape=jax.ShapeDtypeStruct((batch, steps, H_RET, RET_D, RET_D), F32),
        scratch_shapes=[pltpu.VMEM((H_RET, RET_D, RET_D), F32)],
        compiler_params=_cparams(("parallel", "arbitrary")),
        name="retention_fwd_state",
    )(dec, p, p, cos_t, sin_t, tab[:, 3])

    return pl.pallas_call(
        _ret_bwd_kernel,
        grid=(batch, steps),
        in_specs=[
            smem,
            pl.BlockSpec((CHUNK, RET_W), lambda b, s: (bwd_rows(b, s), 0)),
            pl.BlockSpec((CHUNK, RET_W), lambda b, s: (bwd_rows(b, s), 1)),
            pl.BlockSpec((CHUNK, RET_W), lambda b, s: (bwd_rows(b, s), 2)),
            pl.BlockSpec((CHUNK, RET_W), lambda b, s: (bwd_rows(b, s), 3)),
            pl.BlockSpec((CHUNK, RET_D), lambda b, s: (bwd_pos(b, s), 0)),
            pl.BlockSpec((CHUNK, RET_D), lambda b, s: (bwd_pos(b, s), 0)),
            pl.BlockSpec((H_RET, CHUNK, CHUNK), lambda b, s: (0, 0, 0)),
            pl.BlockSpec((H_RET, 3, CHUNK, RET_D), lambda b, s: (0, 0, 0, 0)),
            pl.BlockSpec((None, None, H_RET, RET_D, RET_D), lambda b, s: (b, bwd_state(b, s), 0, 0, 0)),
        ],
        out_specs=pl.BlockSpec((CHUNK, RET_W), lambda b, s: (bwd_rows(b, s), 0)),
        out_shape=jax.ShapeDtypeStruct((n, RET_W), BF16),
        scratch_shapes=[pltpu.VMEM((H_RET, RET_D, RET_D), F32)],
        compiler_params=_cparams(("parallel", "arbitrary")),
        name="retention_out",
    )(dec, p, p, p, p, cos_t, sin_t, dmask, tab[:, :3], s_prev)


def _retention_tables(dec_f, dec_b):
    lgf = jax.nn.log_sigmoid(dec_f.astype(F32))[:, None, None]
    lgb = jax.nn.log_sigmoid(dec_b.astype(F32))[:, None, None]
    i = jnp.arange(CHUNK, dtype=F32)
    diff = i[:, None] - i[None, :]
    k_scale = RET_D ** -0.5
    dmask = jnp.where(diff >= 0, jnp.exp(lgf * jnp.maximum(diff, 0.0)), jnp.exp(lgb * jnp.maximum(-diff, 0.0)))
    dmask = dmask * k_scale
    col = i[None, :, None]
    ones = jnp.ones((1, 1, RET_D), F32)
    xi_f = jnp.exp(lgf * (col + 1.0)) * ones
    xi_b = jnp.exp(lgb * (CHUNK - col)) * ones
    zeta_b = jnp.exp(lgb * col) * (k_scale * ones)
    zeta_f = jnp.exp(lgf * (CHUNK - 1.0 - col)) * (k_scale * ones)
    tab = jnp.stack([xi_f, xi_b, zeta_b, zeta_f], axis=1)
    dec = jnp.stack([jnp.exp(lgf[:, 0, 0] * CHUNK), jnp.exp(lgb[:, 0, 0] * CHUNK)])
    return dmask, tab, dec


def _rope_tables(seq, ctx):
    quarter = RET_D // 4
    inv = ROPE_THETA ** (-jnp.arange(quarter, dtype=F32) / quarter)
    t = jnp.arange(seq)
    rows = (t // GRID_W).astype(F32)
    cols = (t % GRID_W).astype(F32)
    ar = rows[:, None] * inv[None, :]
    ac = cols[:, None] * inv[None, :]
    cos_t = jnp.concatenate([jnp.cos(ar), jnp.cos(ar), jnp.cos(ac), jnp.cos(ac)], axis=1)
    sin_t = jnp.concatenate([-jnp.sin(ar), jnp.sin(ar), -jnp.sin(ac), jnp.sin(ac)], axis=1)
    cos_t = jnp.concatenate([cos_t, jnp.ones((ctx, RET_D), F32)], axis=0)
    sin_t = jnp.concatenate([sin_t, jnp.zeros((ctx, RET_D), F32)], axis=0)
    return cos_t, sin_t


CONV_ROWS = 128
CONV_LANES = 128


def _glu(a, b):
    a = a.astype(F32)
    b = b.astype(F32)
    return a * _sigmoid(b)


def _conv_kernel(a_ref, b_ref, ap_ref, bp_ref, an_ref, bn_ref, w_ref, cb_ref, lg_ref, lb_ref,
                 o_ref, u_scr, y_scr, *, tc, blocks_per_seq, n_lat_blocks):
    i = pl.program_id(0)
    is_ctx = i >= n_lat_blocks
    first = jnp.logical_or(is_ctx, i % blocks_per_seq == 0)
    last = jnp.logical_or(is_ctx, i % blocks_per_seq == blocks_per_seq - 1)
    cw = a_ref.shape[1]
    u_scr[CONV_HALO:CONV_HALO + tc, :] = _glu(a_ref[...], b_ref[...])
    u_scr[0:CONV_HALO, :] = jnp.where(first, 0.0, _glu(ap_ref[...], bp_ref[...]))
    u_scr[CONV_HALO + tc:, :] = jnp.where(last, 0.0, _glu(an_ref[...], bn_ref[...]))

    base = CONV_HALO - CONV_K // 2

    def lane_tile(t, carry):
        lanes = pl.ds(pl.multiple_of(t * CONV_LANES, CONV_LANES), CONV_LANES)
        for r0 in range(0, tc, CONV_ROWS):
            strip = u_scr[r0:r0 + CONV_ROWS + 2 * CONV_HALO, lanes]
            acc = jnp.zeros((CONV_ROWS, CONV_LANES), F32)
            for phase in range(8):
                shifted = strip if phase == 0 else pltpu.roll(strip, strip.shape[0] - phase, axis=0)
                for kk in range(CONV_K):
                    if (base + kk) % 8 == phase:
                        lo = base + kk - phase
                        acc = acc + shifted[lo:lo + CONV_ROWS] * w_ref[kk:kk + 1, lanes]
            y_scr[r0:r0 + CONV_ROWS, lanes] = acc
        return carry

    lax.fori_loop(0, cw // CONV_LANES, lane_tile, 0)

    y = y_scr[...] + cb_ref[...]
    mu = jnp.mean(y, axis=-1, keepdims=True)
    yc = y - mu
    var = jnp.mean(yc * yc, axis=-1, keepdims=True)
    yn = yc * lax.rsqrt(var + LN_EPS) * lg_ref[...] + lb_ref[...]
    o_ref[...] = (yn * _sigmoid(yn)).astype(o_ref.dtype)


def _conv_call(p, w, cb, lg, lb, *, batch, seq, ctx, col_a, col_b):
    n = p.shape[0]
    cw = w.shape[1]
    tc = 256
    assert seq % tc == 0 and ctx == tc
    hb = tc // CONV_HALO
    n_halo = n // CONV_HALO
    n_lat_blocks = batch * seq // tc

    def prev(i):
        return jnp.maximum(i * hb - 1, 0)

    def nxt(i):
        return jnp.minimum((i + 1) * hb, n_halo - 1)

    vec = lambda: pl.BlockSpec((1, cw), lambda i: (0, 0))
    return pl.pallas_call(
        functools.partial(_conv_kernel, tc=tc, blocks_per_seq=seq // tc, n_lat_blocks=n_lat_blocks),
        grid=(n // tc,),
        in_specs=[
            pl.BlockSpec((tc, cw), lambda i: (i, col_a)),
            pl.BlockSpec((tc, cw), lambda i: (i, col_b)),
            pl.BlockSpec((CONV_HALO, cw), lambda i: (prev(i), col_a)),
            pl.BlockSpec((CONV_HALO, cw), lambda i: (prev(i), col_b)),
            pl.BlockSpec((CONV_HALO, cw), lambda i: (nxt(i), col_a)),
            pl.BlockSpec((CONV_HALO, cw), lambda i: (nxt(i), col_b)),
            pl.BlockSpec((CONV_K, cw), lambda i: (0, 0)),
            vec(), vec(), vec(),
        ],
        out_specs=pl.BlockSpec((tc, cw), lambda i: (i, 0)),
        out_shape=jax.ShapeDtypeStruct((n, cw), BF16),
        scratch_shapes=[pltpu.VMEM((tc + 2 * CONV_HALO, cw), F32), pltpu.VMEM((tc, cw), F32)],
        compiler_params=_cparams(("parallel",)),
        name="conv_branch",
    )(p, p, p, p, p, p, w, cb.reshape(1, cw), lg.reshape(1, cw), lb.reshape(1, cw))


def _mixout_kernel(r_ref, u_ref, sr_ref, sc_ref, z_ref, mod_ref, wr_ref, wc_ref, wo_ref, o_ref):
    y_ret = jnp.dot(r_ref[...], wr_ref[...], preferred_element_type=F32)
    y_conv = jnp.dot(u_ref[...], wc_ref[...], preferred_element_type=F32)
    merged = (_sigmoid(sr_ref[...].astype(F32)) * y_ret
              + _sigmoid(sc_ref[...].astype(F32)) * y_conv)
    y = jnp.dot(merged.astype(BF16), wo_ref[...], preferred_element_type=F32)
    o_ref[...] = z_ref[...] + mod_ref[2:3, :] * y


def _mixout_call(r, u, p, z, mod, wr, wc, wo, *, mod_row, col_sr, col_sc):
    n, d = z.shape
    tm = 256
    const = lambda shape: pl.BlockSpec(shape, lambda i: (0, 0), pipeline_mode=pl.Buffered(1))
    return pl.pallas_call(
        _mixout_kernel,
        grid=(n // tm,),
        in_specs=[
            pl.BlockSpec((tm, r.shape[1]), lambda i: (i, 0)),
            pl.BlockSpec((tm, u.shape[1]), lambda i: (i, 0)),
            pl.BlockSpec((tm, d), lambda i: (i, col_sr)),
            pl.BlockSpec((tm, d), lambda i: (i, col_sc)),
            pl.BlockSpec((tm, d), lambda i: (i, 0)),
            pl.BlockSpec((None, N_MOD, d), lambda i: (mod_row(i * tm), 0, 0)),
            const(wr.shape), const(wc.shape), const(wo.shape),
        ],
        out_specs=pl.BlockSpec((tm, d), lambda i: (i, 0)),
        out_shape=jax.ShapeDtypeStruct((n, d), F32),
        compiler_params=_cparams(("parallel",)),
        name="mixer_out_proj",
    )(r, u, p, p, z, mod, wr, wc, wo)


def _router_kernel(z_ref, g_ref, mod_ref, rw_ref, rb_ref, tri_ref, h_ref, idx_ref, gate_ref, rank_ref, cnt_ref,
                   base_scr, *, tm, n_exp):
    @pl.when(pl.program_id(0) == 0)
    def _():
        base_scr[...] = jnp.zeros_like(base_scr)

    g = g_ref[...]
    shift = mod_ref[3:4, :]
    scale = mod_ref[4:5, :]
    rw = rw_ref[...]
    rb = rb_ref[...]
    tri = tri_ref[...]

    def body(r, carry):
        r0 = pl.multiple_of(r * ROW_SUB, ROW_SUB)
        rows = pl.ds(r0, ROW_SUB)
        h = _norm_mod(z_ref[rows, :], g, shift, scale).astype(BF16)
        h_ref[rows, :] = h
        lg = lax.dot_general(rw, h, (((1,), (1,)), ((), ())), preferred_element_type=F32) + rb
        eidx = lax.broadcasted_iota(jnp.int32, lg.shape, 0)
        vals, hits = [], []
        work = lg
        for _ in range(TOP_K):
            m = jnp.max(work, axis=0, keepdims=True)
            sel = jnp.min(jnp.where(work == m, eidx, n_exp), axis=0, keepdims=True)
            hit = eidx == sel
            vals.append(m)
            hits.append(hit)
            work = jnp.where(hit, -jnp.inf, work)
        ex = [jnp.exp(v - vals[0]) for v in vals]
        tot = ex[0] + ex[1] + ex[2] + ex[3]
        chosen = jnp.where(hits[0] | hits[1] | hits[2] | hits[3], 1.0, 0.0)
        before = jnp.dot(chosen.astype(BF16), tri, preferred_element_type=F32) + base_scr[:, 0:1]
        for s in range(TOP_K):
            idx_ref[s:s + 1, rows] = jnp.sum(jnp.where(hits[s], eidx, 0), axis=0, keepdims=True)
            gate_ref[s:s + 1, rows] = ex[s] / tot
            rank_ref[s:s + 1, rows] = jnp.sum(jnp.where(hits[s], before, 0.0), axis=0,
                                              keepdims=True).astype(jnp.int32)
        base_scr[...] += jnp.sum(chosen, axis=1, keepdims=True)
        return carry

    lax.fori_loop(0, tm // ROW_SUB, body, 0)
    cnt_ref[...] = base_scr[...].astype(jnp.int32)


def _router_call(z, n_rows, g, mod, rw_t, rb, *, mod_row, tm):
    d = z.shape[1]
    n_exp = rw_t.shape[0]
    tri = jnp.triu(jnp.ones((ROW_SUB, ROW_SUB), BF16), k=1)
    tok = lambda: pl.BlockSpec((TOP_K, tm), lambda i: (0, i))
    return pl.pallas_call(
        functools.partial(_router_kernel, tm=tm, n_exp=n_exp),
        grid=(n_rows // tm,),
        in_specs=[
            pl.BlockSpec((tm, d), lambda i: (i, 0)),
            pl.BlockSpec((1, d), lambda i: (0, 0)),
            pl.BlockSpec((None, N_MOD, d), lambda i: (mod_row(i * tm), 0, 0)),
            pl.BlockSpec((n_exp, d), lambda i: (0, 0)),
            pl.BlockSpec((n_exp, 1), lambda i: (0, 0)),
            pl.BlockSpec((ROW_SUB, ROW_SUB), lambda i: (0, 0)),
        ],
        out_specs=[
            pl.BlockSpec((tm, d), lambda i: (i, 0)),
            tok(), tok(), tok(),
            pl.BlockSpec((n_exp, 128), lambda i: (0, 0)),
        ],
        out_shape=[
            jax.ShapeDtypeStruct((n_rows, d), BF16),
            jax.ShapeDtypeStruct((TOP_K, n_rows), jnp.int32),
            jax.ShapeDtypeStruct((TOP_K, n_rows), F32),
            jax.ShapeDtypeStruct((TOP_K, n_rows), jnp.int32),
            jax.ShapeDtypeStruct((n_exp, 128), jnp.int32),
        ],
        scratch_shapes=[pltpu.VMEM((n_exp, 128), F32)],
        compiler_params=_cparams(("arbitrary",)),
        name="moe_router",
    )(z, g.reshape(1, d), mod, rw_t, rb.reshape(n_exp, 1), tri)


EXP_TILE = 512


def _cast_rows(src_ref, dst_ref):
    n = src_ref.shape[0]
    step = min(n, ROW_SUB)

    def body(r, carry):
        rows = pl.ds(pl.multiple_of(r * step, step), step)
        dst_ref[rows, :] = src_ref[rows, :].astype(dst_ref.dtype)
        return carry

    lax.fori_loop(0, n // step, body, 0)


def _experts_kernel(te_ref, nu_ref, nxt_ref, x_ref, wg_hbm, bg_ref, wu_hbm, bu_ref, wd_hbm, bd_ref, o_ref,
                    wg_f, wu_f, wd_f, wg_s, wu_s, wd_s, sems, *, layer):
    j = pl.program_id(0)
    active = j < nu_ref[0]
    expert = te_ref[j]
    new_expert = jnp.logical_or(j == 0, expert != te_ref[jnp.maximum(j - 1, 0)])

    def weight_copies(e):
        return (pltpu.make_async_copy(wg_hbm.at[layer, e], wg_f, sems.at[0]),
                pltpu.make_async_copy(wu_hbm.at[layer, e], wu_f, sems.at[1]),
                pltpu.make_async_copy(wd_hbm.at[layer, e], wd_f, sems.at[2]))

    @pl.when(jnp.logical_and(active, j == 0))
    def _():
        for cp in weight_copies(expert):
            cp.start()

    @pl.when(jnp.logical_and(active, new_expert))
    def _():
        for cp in weight_copies(expert):
            cp.wait()
        _cast_rows(wg_f, wg_s)
        _cast_rows(wu_f, wu_s)
        _cast_rows(wd_f, wd_s)
        nxt = nxt_ref[expert]

        @pl.when(nxt >= 0)
        def _():
            for cp in weight_copies(nxt):
                cp.start()

    @pl.when(active)
    def _():
        x = x_ref[...]
        a = jnp.minimum(jnp.dot(x, wg_s[...], preferred_element_type=F32) + bg_ref[...], SWIGLU_LIMIT)
        b = jnp.clip(jnp.dot(x, wu_s[...], preferred_element_type=F32) + bu_ref[...],
                     -SWIGLU_LIMIT, SWIGLU_LIMIT)
        act = (b + 1.0) * a * _sigmoid(SWIGLU_ALPHA * a)
        y = jnp.dot(act.astype(BF16), wd_s[...], preferred_element_type=F32) + bd_ref[...]
        o_ref[...] = y.astype(o_ref.dtype)

    @pl.when(jnp.logical_not(active))
    def _():
        o_ref[...] = jnp.zeros_like(o_ref)


def _experts_call(tile_expert, n_used, next_expert, xs, layer, wg, bg, wu, bu, wd, bd):
    rows, d = xs.shape
    depth, n_exp, _, de = wg.shape
    pick = lambda j, te, nu, nxt: (layer, te[j], 0, 0)
    hbm = pl.BlockSpec(memory_space=pl.ANY)
    grid_spec = pltpu.PrefetchScalarGridSpec(
        num_scalar_prefetch=3,
        grid=(rows // EXP_TILE,),
        in_specs=[
            pl.BlockSpec((EXP_TILE, d), lambda j, te, nu, nxt: (j, 0)),
            hbm,
            pl.BlockSpec((None, None, 1, de), pick),
            hbm,
            pl.BlockSpec((None, None, 1, de), pick),
            hbm,
            pl.BlockSpec((None, None, 1, d), pick),
        ],
        out_specs=pl.BlockSpec((EXP_TILE, d), lambda j, te, nu, nxt: (j, 0)),
        scratch_shapes=[pltpu.VMEM((d, de), F32), pltpu.VMEM((d, de), F32), pltpu.VMEM((de, d), F32),
                        pltpu.VMEM((d, de), BF16), pltpu.VMEM((d, de), BF16), pltpu.VMEM((de, d), BF16),
                        pltpu.SemaphoreType.DMA((3,))],
    )
    return pl.pallas_call(
        functools.partial(_experts_kernel, layer=layer),
        grid_spec=grid_spec,
        out_shape=jax.ShapeDtypeStruct((rows, d), BF16),
        compiler_params=_cparams(("arbitrary",)),
        name="moe_experts",
    )(tile_expert, n_used, next_expert, xs, wg, bg.reshape(depth, n_exp, 1, de), wu,
      bu.reshape(depth, n_exp, 1, de), wd, bd.reshape(depth, n_exp, 1, d))


def _combine_kernel(y_ref, p_ref, z_ref, mod_ref, fg_ref, o_ref, *, final):
    p = p_ref[...]
    acc = p[:, 0:1] * y_ref[0].astype(F32)
    for s in range(1, TOP_K):
        acc += p[:, s:s + 1] * y_ref[s].astype(F32)
    z = z_ref[...] + mod_ref[5:6, :] * acc
    if final:
        ms = jnp.mean(z * z, axis=-1, keepdims=True)
        z = z * lax.rsqrt(ms + RMS_EPS) * fg_ref[...]
    o_ref[...] = z


def _combine_call(yg, gates_t, z, n_rows, mod, fg, *, mod_row, final):
    d = z.shape[1]
    tm = 256
    return pl.pallas_call(
        functools.partial(_combine_kernel, final=final),
        grid=(n_rows // tm,),
        in_specs=[
            pl.BlockSpec((TOP_K, tm, d), lambda i: (0, i, 0)),
            pl.BlockSpec((tm, TOP_K), lambda i: (i, 0)),
            pl.BlockSpec((tm, d), lambda i: (i, 0)),
            pl.BlockSpec((None, N_MOD, d), lambda i: (mod_row(i * tm), 0, 0)),
            pl.BlockSpec((1, d), lambda i: (0, 0)),
        ],
        out_specs=pl.BlockSpec((tm, d), lambda i: (i, 0)),
        out_shape=jax.ShapeDtypeStruct((n_rows, d), F32),
        compiler_params=_cparams(("parallel",)),
        name="moe_combine",
    )(yg, gates_t, z, mod, fg.reshape(1, d))


def _invert_kernel(pos_ref, init_hbm, out_ref, *, n_tok, n_slots):
    pltpu.sync_copy(init_hbm, out_ref)
    for s in range(n_slots):
        def body(t, carry, s=s):
            out_ref[pos_ref[s * n_tok + t]] = t
            return carry

        lax.fori_loop(0, n_tok, body, 0, unroll=8)


def _invert_call(pos, n_rows):
    k, n = pos.shape
    smem = pl.BlockSpec(memory_space=pltpu.SMEM)
    return pl.pallas_call(
        functools.partial(_invert_kernel, n_tok=n, n_slots=k),
        in_specs=[smem, pl.BlockSpec(memory_space=pl.ANY)],
        out_specs=smem,
        out_shape=jax.ShapeDtypeStruct((n_rows,), jnp.int32),
        name="moe_row_tokens",
    )(pos.reshape(-1), jnp.arange(n_rows, dtype=jnp.int32) % n)


def _dispatch_plan(idx, rank, counts):
    k, n = idx.shape
    n_exp = counts.shape[0]
    padded = ((counts + EXP_TILE - 1) // EXP_TILE) * EXP_TILE
    ends = jnp.cumsum(padded)
    offsets = ends - padded
    onehot = idx[:, :, None] == jnp.arange(n_exp, dtype=jnp.int32)[None, None, :]
    pos = rank + jnp.sum(jnp.where(onehot, offsets[None, None, :], 0), axis=-1)
    rows = k * n + n_exp * EXP_TILE
    row_token = _invert_call(pos, rows)
    n_tiles = rows // EXP_TILE
    tile_start = jnp.arange(n_tiles, dtype=jnp.int32) * EXP_TILE
    n_used = (ends[-1] // EXP_TILE).astype(jnp.int32)
    last_start = jnp.maximum(n_used - 1, 0) * EXP_TILE
    tile_expert = jnp.sum((ends[None, :] <= jnp.minimum(tile_start, last_start)[:, None]).astype(jnp.int32), axis=1)
    tile_expert = jnp.minimum(tile_expert, n_exp - 1)
    e_ids = jnp.arange(n_exp, dtype=jnp.int32)
    later_used = jnp.logical_and(e_ids[None, :] > e_ids[:, None], (counts > 0)[None, :])
    next_expert = jnp.min(jnp.where(later_used, e_ids[None, :], n_exp), axis=1)
    next_expert = jnp.where(next_expert == n_exp, -1, next_expert).astype(jnp.int32)
    return pos, row_token, tile_expert, n_used.reshape(1), next_expert


def kernel(x, c, ctx, c_ctx, ada_w, ada_b, norm1_g, w_in, ret_decay_fwd, ret_decay_bwd, w_ret_o,
           conv_dw_w, conv_dw_b, conv_ln_g, conv_ln_b, w_conv_o, w_out, norm2_g, router_w, router_b,
           exp_w_gate, exp_b_gate, exp_w_up, exp_b_up, exp_w_down, exp_b_down, final_norm_g):
    batch, seq, d = x.shape
    n_ctx = ctx.shape[1]
    depth = ada_w.shape[0]
    n_exp = router_w.shape[2]
    assert batch + 1 <= MOD_ROWS and seq % CHUNK == 0 and n_ctx == CHUNK
    n_groups = 2 if batch % 2 == 0 else 1
    gb = batch // n_groups
    n_lat = gb * seq
    tm_big = _pick_tile(seq, 1024)
    while (gb * n_ctx) % tm_big:
        tm_big //= 2

    def make_mod_row(g):
        return lambda row0: jnp.where(row0 >= n_lat, batch, g * gb + row0 // seq)

    mod_rows = [make_mod_row(g) for g in range(n_groups)]
    cond = jnp.concatenate([c, c_ctx[None, :], jnp.zeros((MOD_ROWS - batch - 1, d), F32)], axis=0)
    mod_all = _ada_call(cond, ada_w, ada_b).reshape(depth, MOD_ROWS, N_MOD, d)
    cos_t, sin_t = _rope_tables(seq, n_ctx)
    cw = conv_dw_w.shape[2]
    col0 = 4 * RET_W // cw
    s_col = (4 * RET_W + 2 * cw) // d

    zs = [jnp.concatenate([x[g * gb:(g + 1) * gb].reshape(n_lat, d),
                           ctx[g * gb:(g + 1) * gb].reshape(gb * n_ctx, d)], axis=0) for g in range(n_groups)]

    for l in range(depth):
        mod = mod_all[l]
        last = l == depth - 1
        w_in_l = w_in[l].astype(BF16)
        w_ret_l = w_ret_o[l].astype(BF16)
        w_conv_l = w_conv_o[l].astype(BF16)
        w_out_l = w_out[l].astype(BF16)
        rw_t = router_w[l].T.astype(BF16)
        dmask, tab, dec = _retention_tables(ret_decay_fwd[l], ret_decay_bwd[l])
        staged = []
        for g in range(n_groups):
            z, mod_row = zs[g], mod_rows[g]
            p = _inproj_call(z, norm1_g[l], mod, w_in_l, mod_row=mod_row, tm=tm_big)
            r = _retention_call(p, dec, cos_t, sin_t, dmask, tab, batch=gb, seq=seq, ctx=n_ctx)
            u = _conv_call(p, conv_dw_w[l], conv_dw_b[l], conv_ln_g[l], conv_ln_b[l],
                           batch=gb, seq=seq, ctx=n_ctx, col_a=col0, col_b=col0 + 1)
            z = _mixout_call(r, u, p, z, mod, w_ret_l, w_conv_l, w_out_l, mod_row=mod_row,
                             col_sr=s_col, col_sc=s_col + 1)
            n_rows = n_lat if last else z.shape[0]
            h, idx, gates, rank, counts = _router_call(z, n_rows, norm2_g[l], mod, rw_t, router_b[l],
                                                       mod_row=mod_row, tm=tm_big)
            pos, row_token, tile_expert, n_used, next_expert = _dispatch_plan(idx, rank, counts[:, 0])
            if g == 0 and l > 0:
                row_token = _after(row_token, zs[n_groups - 1])
            xs = h.at[row_token].get(mode="promise_in_bounds")
            staged.append((z, n_rows, gates, pos, tile_expert, n_used, next_expert, xs))

        def combine(g, yg):
            z, n_rows, gates = staged[g][:3]
            return _combine_call(yg, gates.T, z, n_rows, mod, final_norm_g, mod_row=mod_rows[g], final=last)

        yg_prev = None
        for g in range(n_groups):
            z, n_rows, gates, pos, tile_expert, n_used, next_expert, xs = staged[g]
            ys = _experts_call(tile_expert, n_used, next_expert, xs, l, exp_w_gate, exp_b_gate, exp_w_up, exp_b_up,
                               exp_w_down, exp_b_down)
            if g > 0:
                zs[g - 1] = combine(g - 1, yg_prev)
                pos = _after(pos, zs[g - 1])
            yg_prev = ys.at[pos].get(mode="promise_in_bounds").reshape(TOP_K, n_rows, d)
        zs[n_groups - 1] = combine(n_groups - 1, yg_prev)
    return jnp.concatenate([z.reshape(gb, seq, d) for z in zs], axis=0)
```

```python
import functools

import jax
import jax.numpy as jnp
from jax import lax
from jax.experimental import pallas as pl
from jax.experimental.pallas import tpu as pltpu

F32 = jnp.float32
BF16 = jnp.bfloat16

GRID_W = 64
H_RET = 8
RET_D = 128
RET_W = H_RET * RET_D
ROPE_THETA = 10000.0
CONV_K = 31
CONV_HALO = 16
TOP_K = 4
SWIGLU_LIMIT = 7.0
SWIGLU_ALPHA = 1.702
N_MOD = 6
RMS_EPS = 1e-6
LN_EPS = 1e-5
MOD_ROWS = 16

CHUNK = 256
ROW_SUB = 256
VMEM_LIMIT = 56 * 1024 * 1024


def _cparams(sem):
    return pltpu.CompilerParams(dimension_semantics=sem, vmem_limit_bytes=VMEM_LIMIT)


def _pick_tile(n, pref):
    t = min(n, pref)
    while n % t:
        t //= 2
    return t


def _after(x, dep):
    x, _ = lax.optimization_barrier((x, dep))
    return x


def _sigmoid(x):
    return 0.5 * jnp.tanh(0.5 * x) + 0.5


def _ada_kernel(cond_ref, w_ref, b_ref, o_ref):
    c = cond_ref[...]
    s = (c * _sigmoid(c)).astype(BF16)
    o_ref[...] = jnp.dot(s, w_ref[...].astype(BF16), preferred_element_type=F32) + b_ref[...]


def _ada_call(cond16, ada_w, ada_b):
    depth, d, nw = ada_w.shape
    tn = _pick_tile(nw, 1024)
    return pl.pallas_call(
        _ada_kernel,
        grid=(depth, nw // tn),
        in_specs=[
            pl.BlockSpec((MOD_ROWS, d), lambda l, j: (0, 0)),
            pl.BlockSpec((None, d, tn), lambda l, j: (l, 0, j)),
            pl.BlockSpec((None, 1, tn), lambda l, j: (l, 0, j)),
        ],
        out_specs=pl.BlockSpec((None, MOD_ROWS, tn), lambda l, j: (l, 0, j)),
        out_shape=jax.ShapeDtypeStruct((depth, MOD_ROWS, nw), F32),
        compiler_params=_cparams(("parallel", "parallel")),
        name="ada_mod",
    )(cond16, ada_w, ada_b.reshape(depth, 1, nw))


def _norm_mod(x, g, shift, scale):
    ms = jnp.mean(x * x, axis=-1, keepdims=True)
    y = x * lax.rsqrt(ms + RMS_EPS) * g
    return y * (1.0 + scale) + shift


def _inproj_kernel(z_ref, g_ref, mod_ref, w_ref, o_ref, h_scr, *, tm):
    @pl.when(pl.program_id(1) == 0)
    def _():
        g = g_ref[...]
        shift = mod_ref[0:1, :]
        scale = mod_ref[1:2, :]

        def body(r, carry):
            rows = pl.ds(pl.multiple_of(r * ROW_SUB, ROW_SUB), ROW_SUB)
            h_scr[rows, :] = _norm_mod(z_ref[rows, :], g, shift, scale).astype(BF16)
            return carry

        lax.fori_loop(0, tm // ROW_SUB, body, 0)

    o_ref[...] = jnp.dot(h_scr[...], w_ref[...], preferred_element_type=F32).astype(o_ref.dtype)


def _inproj_call(z, g, mod, w, *, mod_row, tm):
    n, d = z.shape
    nw = w.shape[1]
    tn = _pick_tile(nw, 1024)
    return pl.pallas_call(
        functools.partial(_inproj_kernel, tm=tm),
        grid=(n // tm, nw // tn),
        in_specs=[
            pl.BlockSpec((tm, d), lambda i, j: (i, 0)),
            pl.BlockSpec((1, d), lambda i, j: (0, 0)),
            pl.BlockSpec((None, N_MOD, d), lambda i, j: (mod_row(i * tm), 0, 0)),
            pl.BlockSpec((d, tn), lambda i, j: (0, j)),
        ],
        out_specs=pl.BlockSpec((tm, tn), lambda i, j: (i, j)),
        out_shape=jax.ShapeDtypeStruct((n, nw), BF16),
        scratch_shapes=[pltpu.VMEM((tm, d), BF16)],
        compiler_params=_cparams(("parallel", "arbitrary")),
        name="mixer_in_proj",
    )(z, g.reshape(1, d), mod, w)


def _rope(x, rope_ref):
    return (x * rope_ref[0] + pltpu.roll(x, 96, axis=1) * rope_ref[1]
            + pltpu.roll(x, 32, axis=1) * rope_ref[2])


def _ret_fwd_kernel(dec_ref, k_ref, v_ref, rope_ref, zf_ref, s_out_ref, s_scr):
    @pl.when(pl.program_id(1) == 0)
    def _():
        s_scr[...] = jnp.zeros_like(s_scr)

    for h in range(H_RET):
        cols = slice(h * RET_D, (h + 1) * RET_D)
        s_prev = s_scr[h]
        s_out_ref[h] = s_prev
        kz = _rope(k_ref[:, cols], rope_ref) * zf_ref[h]
        ds = lax.dot_general(kz, v_ref[:, cols], (((0,), (0,)), ((), ())), preferred_element_type=F32)
        s_scr[h] = dec_ref[0, h] * s_prev + ds


def _ret_bwd_kernel(dec_ref, q_ref, k_ref, v_ref, g_ref, rope_ref, dm_ref, tab_ref, sf_ref,
                    o_ref, s_scr):
    @pl.when(pl.program_id(1) == 0)
    def _():
        s_scr[...] = jnp.zeros_like(s_scr)

    for h in range(H_RET):
        cols = slice(h * RET_D, (h + 1) * RET_D)
        q = _rope(q_ref[:, cols], rope_ref)
        k = _rope(k_ref[:, cols], rope_ref)
        v = v_ref[:, cols]
        s = lax.dot_general(q, k, (((1,), (1,)), ((), ())), preferred_element_type=F32)
        a = (s * dm_ref[h]).astype(BF16)
        sb = s_scr[h]
        o = jnp.dot(a, v, preferred_element_type=F32)
        o += jnp.dot(q * tab_ref[h, 0], sf_ref[h].astype(BF16), preferred_element_type=F32)
        o += jnp.dot(q * tab_ref[h, 1], sb.astype(BF16), preferred_element_type=F32)
        kz = k * tab_ref[h, 2]
        ds = lax.dot_general(kz, v, (((0,), (0,)), ((), ())), preferred_element_type=F32)
        s_scr[h] = dec_ref[1, h] * sb + ds
        mu = jnp.mean(o, axis=-1, keepdims=True)
        oc = o - mu
        var = jnp.mean(oc * oc, axis=-1, keepdims=True)
        gate = g_ref[:, cols].astype(F32)
        o_ref[:, cols] = (oc * lax.rsqrt(var + LN_EPS) * (gate * _sigmoid(gate))).astype(o_ref.dtype)


def _retention_call(p, dec, rope_t, dmask, tab, *, batch, seq, ctx):
    n = p.shape[0]
    nl = seq // CHUNK
    nlb = batch * nl
    steps = nl + 1
    assert ctx == CHUNK

    def fwd_rows(b, s):
        return jnp.where(s == 0, nlb + b, b * nl + s - 1)

    def fwd_pos(b, s):
        return jnp.where(s == 0, nl, s - 1)

    def bwd_rows(b, s):
        return jnp.where(s == 0, nlb + b, b * nl + nl - s)

    def bwd_pos(b, s):
        return jnp.where(s == 0, nl, nl - s)

    def bwd_state(b, s):
        return jnp.where(s == 0, 0, nl - s + 1)

    smem = pl.BlockSpec(memory_space=pltpu.SMEM)
    s_prev = pl.pallas_call(
        _ret_fwd_kernel,
        grid=(batch, steps),
        in_specs=[
            smem,
            pl.BlockSpec((CHUNK, RET_W), lambda b, s: (fwd_rows(b, s), 1)),
            pl.BlockSpec((CHUNK, RET_W), lambda b, s: (fwd_rows(b, s), 2)),
            pl.BlockSpec((3, CHUNK, RET_D), lambda b, s: (0, fwd_pos(b, s), 0)),
            pl.BlockSpec((H_RET, CHUNK, RET_D), lambda b, s: (0, 0, 0)),
        ],
        out_specs=pl.BlockSpec((None, None, H_RET, RET_D, RET_D), lambda b, s: (b, s, 0, 0, 0)),
        out_shape=jax.ShapeDtypeStruct((batch, steps, H_RET, RET_D, RET_D), F32),
        scratch_shapes=[pltpu.VMEM((H_RET, RET_D, RET_D), F32)],
        compiler_params=_cparams(("parallel", "arbitrary")),
        name="retention_fwd_state",
    )(dec, p, p, rope_t, tab[:, 3])

    return pl.pallas_call(
        _ret_bwd_kernel,
        grid=(batch, steps),
        in_specs=[
            smem,
            pl.BlockSpec((CHUNK, RET_W), lambda b, s: (bwd_rows(b, s), 0)),
            pl.BlockSpec((CHUNK, RET_W), lambda b, s: (bwd_rows(b, s), 1)),
            pl.BlockSpec((CHUNK, RET_W), lambda b, s: (bwd_rows(b, s), 2)),
            pl.BlockSpec((CHUNK, RET_W), lambda b, s: (bwd_rows(b, s), 3)),
            pl.BlockSpec((3, CHUNK, RET_D), lambda b, s: (0, bwd_pos(b, s), 0)),
            pl.BlockSpec((H_RET, CHUNK, CHUNK), lambda b, s: (0, 0, 0)),
            pl.BlockSpec((H_RET, 3, CHUNK, RET_D), lambda b, s: (0, 0, 0, 0)),
            pl.BlockSpec((None, None, H_RET, RET_D, RET_D), lambda b, s: (b, bwd_state(b, s), 0, 0, 0)),
        ],
        out_specs=pl.BlockSpec((CHUNK, RET_W), lambda b, s: (bwd_rows(b, s), 0)),
        out_shape=jax.ShapeDtypeStruct((n, RET_W), BF16),
        scratch_shapes=[pltpu.VMEM((H_RET, RET_D, RET_D), F32)],
        compiler_params=_cparams(("parallel", "arbitrary")),
        name="retention_out",
    )(dec, p, p, p, p, rope_t, dmask, tab[:, :3], s_prev)


def _retention_tables(dec_f, dec_b):
    lgf = jax.nn.log_sigmoid(dec_f.astype(F32))[:, None, None]
    lgb = jax.nn.log_sigmoid(dec_b.astype(F32))[:, None, None]
    i = jnp.arange(CHUNK, dtype=F32)
    diff = i[:, None] - i[None, :]
    k_scale = RET_D ** -0.5
    dmask = jnp.where(diff >= 0, jnp.exp(lgf * jnp.maximum(diff, 0.0)), jnp.exp(lgb * jnp.maximum(-diff, 0.0)))
    dmask = dmask * k_scale
    col = i[None, :, None]
    ones = jnp.ones((1, 1, RET_D), F32)
    xi_f = jnp.exp(lgf * (col + 1.0)) * ones
    xi_b = jnp.exp(lgb * (CHUNK - col)) * ones
    zeta_b = jnp.exp(lgb * col) * (k_scale * ones)
    zeta_f = jnp.exp(lgf * (CHUNK - 1.0 - col)) * (k_scale * ones)
    tab = jnp.stack([xi_f, xi_b, zeta_b, zeta_f], axis=1).astype(BF16)
    dec = jnp.stack([jnp.exp(lgf[:, 0, 0] * CHUNK), jnp.exp(lgb[:, 0, 0] * CHUNK)])
    return dmask, tab, dec


def _rope_tables(seq, ctx):
    quarter = RET_D // 4
    inv = ROPE_THETA ** (-jnp.arange(quarter, dtype=F32) / quarter)
    t = jnp.arange(seq)
    rows = (t // GRID_W).astype(F32)
    cols = (t % GRID_W).astype(F32)
    ar = rows[:, None] * inv[None, :]
    ac = cols[:, None] * inv[None, :]
    zero = jnp.zeros_like(ar)
    cos_t = jnp.concatenate([jnp.cos(ar), jnp.cos(ar), jnp.cos(ac), jnp.cos(ac)], axis=1)
    sin_up = jnp.concatenate([-jnp.sin(ar), zero, -jnp.sin(ac), zero], axis=1)
    sin_dn = jnp.concatenate([zero, jnp.sin(ar), zero, jnp.sin(ac)], axis=1)
    pad = lambda t, fill: jnp.concatenate([t, jnp.full((ctx, RET_D), fill, F32)], axis=0)
    return jnp.stack([pad(cos_t, 1.0), pad(sin_up, 0.0), pad(sin_dn, 0.0)]).astype(BF16)


CONV_ROWS = 128
CONV_LANES = 128


def _glu(a, b):
    a = a.astype(F32)
    b = b.astype(F32)
    return a * _sigmoid(b)


def _conv_kernel(a_ref, b_ref, ap_ref, bp_ref, an_ref, bn_ref, w_ref, cb_ref, lg_ref, lb_ref,
                 o_ref, u_scr, y_scr, *, tc, blocks_per_seq, n_lat_blocks):
    i = pl.program_id(0)
    is_ctx = i >= n_lat_blocks
    first = jnp.logical_or(is_ctx, i % blocks_per_seq == 0)
    last = jnp.logical_or(is_ctx, i % blocks_per_seq == blocks_per_seq - 1)
    cw = a_ref.shape[1]
    u_scr[CONV_HALO:CONV_HALO + tc, :] = _glu(a_ref[...], b_ref[...])
    u_scr[0:CONV_HALO, :] = jnp.where(first, 0.0, _glu(ap_ref[...], bp_ref[...]))
    u_scr[CONV_HALO + tc:, :] = jnp.where(last, 0.0, _glu(an_ref[...], bn_ref[...]))

    base = CONV_HALO - CONV_K // 2

    def lane_tile(t, carry):
        lanes = pl.ds(pl.multiple_of(t * CONV_LANES, CONV_LANES), CONV_LANES)
        for r0 in range(0, tc, CONV_ROWS):
            strip = u_scr[r0:r0 + CONV_ROWS + 2 * CONV_HALO, lanes]
            acc = jnp.zeros((CONV_ROWS, CONV_LANES), F32)
            for phase in range(8):
                shifted = strip if phase == 0 else pltpu.roll(strip, strip.shape[0] - phase, axis=0)
                for kk in range(CONV_K):
                    if (base + kk) % 8 == phase:
                        lo = base + kk - phase
                        acc = acc + shifted[lo:lo + CONV_ROWS] * w_ref[kk:kk + 1, lanes]
            y_scr[r0:r0 + CONV_ROWS, lanes] = acc
        return carry

    lax.fori_loop(0, cw // CONV_LANES, lane_tile, 0)

    y = y_scr[...] + cb_ref[...]
    mu = jnp.mean(y, axis=-1, keepdims=True)
    yc = y - mu
    var = jnp.mean(yc * yc, axis=-1, keepdims=True)
    yn = yc * lax.rsqrt(var + LN_EPS) * lg_ref[...] + lb_ref[...]
    o_ref[...] = (yn * _sigmoid(yn)).astype(o_ref.dtype)


def _conv_call(p, w, cb, lg, lb, *, batch, seq, ctx, col_a, col_b):
    n = p.shape[0]
    cw = w.shape[1]
    tc = 256
    assert seq % tc == 0 and ctx == tc
    hb = tc // CONV_HALO
    n_halo = n // CONV_HALO
    n_lat_blocks = batch * seq // tc

    def prev(i):
        return jnp.maximum(i * hb - 1, 0)

    def nxt(i):
        return jnp.minimum((i + 1) * hb, n_halo - 1)

    vec = lambda: pl.BlockSpec((1, cw), lambda i: (0, 0))
    return pl.pallas_call(
        functools.partial(_conv_kernel, tc=tc, blocks_per_seq=seq // tc, n_lat_blocks=n_lat_blocks),
        grid=(n // tc,),
        in_specs=[
            pl.BlockSpec((tc, cw), lambda i: (i, col_a)),
            pl.BlockSpec((tc, cw), lambda i: (i, col_b)),
            pl.BlockSpec((CONV_HALO, cw), lambda i: (prev(i), col_a)),
            pl.BlockSpec((CONV_HALO, cw), lambda i: (prev(i), col_b)),
            pl.BlockSpec((CONV_HALO, cw), lambda i: (nxt(i), col_a)),
            pl.BlockSpec((CONV_HALO, cw), lambda i: (nxt(i), col_b)),
            pl.BlockSpec((CONV_K, cw), lambda i: (0, 0)),
            vec(), vec(), vec(),
        ],
        out_specs=pl.BlockSpec((tc, cw), lambda i: (i, 0)),
        out_shape=jax.ShapeDtypeStruct((n, cw), BF16),
        scratch_shapes=[pltpu.VMEM((tc + 2 * CONV_HALO, cw), F32), pltpu.VMEM((tc, cw), F32)],
        compiler_params=_cparams(("parallel",)),
        name="conv_branch",
    )(p, p, p, p, p, p, w, cb.reshape(1, cw), lg.reshape(1, cw), lb.reshape(1, cw))


def _mixout_kernel(r_ref, u_ref, sr_ref, sc_ref, z_ref, mod_ref, wr_ref, wc_ref, wo_ref, o_ref):
    y_ret = jnp.dot(r_ref[...], wr_ref[...], preferred_element_type=F32)
    y_conv = jnp.dot(u_ref[...], wc_ref[...], preferred_element_type=F32)
    merged = (_sigmoid(sr_ref[...].astype(F32)) * y_ret
              + _sigmoid(sc_ref[...].astype(F32)) * y_conv)
    y = jnp.dot(merged.astype(BF16), wo_ref[...], preferred_element_type=F32)
    o_ref[...] = z_ref[...] + mod_ref[2:3, :] * y


def _mixout_call(r, u, p, z, mod, wr, wc, wo, *, mod_row, col_sr, col_sc):
    n, d = z.shape
    tm = 256
    const = lambda shape: pl.BlockSpec(shape, lambda i: (0, 0), pipeline_mode=pl.Buffered(1))
    return pl.pallas_call(
        _mixout_kernel,
        grid=(n // tm,),
        in_specs=[
            pl.BlockSpec((tm, r.shape[1]), lambda i: (i, 0)),
            pl.BlockSpec((tm, u.shape[1]), lambda i: (i, 0)),
            pl.BlockSpec((tm, d), lambda i: (i, col_sr)),
            pl.BlockSpec((tm, d), lambda i: (i, col_sc)),
            pl.BlockSpec((tm, d), lambda i: (i, 0)),
            pl.BlockSpec((None, N_MOD, d), lambda i: (mod_row(i * tm), 0, 0)),
            const(wr.shape), const(wc.shape), const(wo.shape),
        ],
        out_specs=pl.BlockSpec((tm, d), lambda i: (i, 0)),
        out_shape=jax.ShapeDtypeStruct((n, d), F32),
        compiler_params=_cparams(("parallel",)),
        name="mixer_out_proj",
    )(r, u, p, p, z, mod, wr, wc, wo)


def _router_kernel(z_ref, g_ref, mod_ref, rw_ref, rb_ref, tri_ref, h_ref, idx_ref, gate_ref, rank_ref, cnt_ref,
                   base_scr, *, tm, n_exp):
    @pl.when(pl.program_id(0) == 0)
    def _():
        base_scr[...] = jnp.zeros_like(base_scr)

    g = g_ref[...]
    shift = mod_ref[3:4, :]
    scale = mod_ref[4:5, :]
    rw = rw_ref[...]
    rb = rb_ref[...]
    tri = tri_ref[...]

    def body(r, carry):
        r0 = pl.multiple_of(r * ROW_SUB, ROW_SUB)
        rows = pl.ds(r0, ROW_SUB)
        h = _norm_mod(z_ref[rows, :], g, shift, scale).astype(BF16)
        h_ref[rows, :] = h
        lg = lax.dot_general(rw, h, (((1,), (1,)), ((), ())), preferred_element_type=F32) + rb
        eidx = lax.broadcasted_iota(jnp.int32, lg.shape, 0)
        vals, hits = [], []
        work = lg
        for _ in range(TOP_K):
            m = jnp.max(work, axis=0, keepdims=True)
            sel = jnp.min(jnp.where(work == m, eidx, n_exp), axis=0, keepdims=True)
            hit = eidx == sel
            vals.append(m)
            hits.append(hit)
            work = jnp.where(hit, -jnp.inf, work)
        ex = [jnp.exp(v - vals[0]) for v in vals]
        tot = ex[0] + ex[1] + ex[2] + ex[3]
        chosen = jnp.where(hits[0] | hits[1] | hits[2] | hits[3], 1.0, 0.0)
        before = jnp.dot(chosen.astype(BF16), tri, preferred_element_type=F32) + base_scr[:, 0:1]
        for s in range(TOP_K):
            idx_ref[s:s + 1, rows] = jnp.sum(jnp.where(hits[s], eidx, 0), axis=0, keepdims=True)
            gate_ref[s:s + 1, rows] = ex[s] / tot
            rank_ref[s:s + 1, rows] = jnp.sum(jnp.where(hits[s], before, 0.0), axis=0,
                                              keepdims=True).astype(jnp.int32)
        base_scr[...] += jnp.sum(chosen, axis=1, keepdims=True)
        return carry

    lax.fori_loop(0, tm // ROW_SUB, body, 0)
    cnt_ref[...] = base_scr[...].astype(jnp.int32)


def _router_call(z, n_rows, g, mod, rw_t, rb, *, mod_row, tm):
    d = z.shape[1]
    n_exp = rw_t.shape[0]
    tri = jnp.triu(jnp.ones((ROW_SUB, ROW_SUB), BF16), k=1)
    tok = lambda: pl.BlockSpec((TOP_K, tm), lambda i: (0, i))
    return pl.pallas_call(
        functools.partial(_router_kernel, tm=tm, n_exp=n_exp),
        grid=(n_rows // tm,),
        in_specs=[
            pl.BlockSpec((tm, d), lambda i: (i, 0)),
            pl.BlockSpec((1, d), lambda i: (0, 0)),
            pl.BlockSpec((None, N_MOD, d), lambda i: (mod_row(i * tm), 0, 0)),
            pl.BlockSpec((n_exp, d), lambda i: (0, 0)),
            pl.BlockSpec((n_exp, 1), lambda i: (0, 0)),
            pl.BlockSpec((ROW_SUB, ROW_SUB), lambda i: (0, 0)),
        ],
        out_specs=[
            pl.BlockSpec((tm, d), lambda i: (i, 0)),
            tok(), tok(), tok(),
            pl.BlockSpec((n_exp, 128), lambda i: (0, 0)),
        ],
        out_shape=[
            jax.ShapeDtypeStruct((n_rows, d), BF16),
            jax.ShapeDtypeStruct((TOP_K, n_rows), jnp.int32),
            jax.ShapeDtypeStruct((TOP_K, n_rows), F32),
            jax.ShapeDtypeStruct((TOP_K, n_rows), jnp.int32),
            jax.ShapeDtypeStruct((n_exp, 128), jnp.int32),
        ],
        scratch_shapes=[pltpu.VMEM((n_exp, 128), F32)],
        compiler_params=_cparams(("arbitrary",)),
        name="moe_router",
    )(z, g.reshape(1, d), mod, rw_t, rb.reshape(n_exp, 1), tri)


EXP_TILE = 512


def _cast_rows(src_ref, dst_ref):
    n = src_ref.shape[0]
    step = min(n, ROW_SUB)

    def body(r, carry):
        rows = pl.ds(pl.multiple_of(r * step, step), step)
        dst_ref[rows, :] = src_ref[rows, :].astype(dst_ref.dtype)
        return carry

    lax.fori_loop(0, n // step, body, 0)


def _experts_kernel(te_ref, nu_ref, nxt_ref, x_ref, wg_hbm, bg_ref, wu_hbm, bu_ref, wd_hbm, bd_ref, o_ref,
                    wg_f, wu_f, wd_f, wg_s, wu_s, wd_s, sems, *, layer):
    j = pl.program_id(0)
    active = j < nu_ref[0]
    expert = te_ref[j]
    new_expert = jnp.logical_or(j == 0, expert != te_ref[jnp.maximum(j - 1, 0)])

    def weight_copies(e):
        return (pltpu.make_async_copy(wg_hbm.at[layer, e], wg_f, sems.at[0]),
                pltpu.make_async_copy(wu_hbm.at[layer, e], wu_f, sems.at[1]),
                pltpu.make_async_copy(wd_hbm.at[layer, e], wd_f, sems.at[2]))

    @pl.when(jnp.logical_and(active, j == 0))
    def _():
        for cp in weight_copies(expert):
            cp.start()

    @pl.when(jnp.logical_and(active, new_expert))
    def _():
        for cp in weight_copies(expert):
            cp.wait()
        _cast_rows(wg_f, wg_s)
        _cast_rows(wu_f, wu_s)
        _cast_rows(wd_f, wd_s)
        nxt = nxt_ref[expert]

        @pl.when(nxt >= 0)
        def _():
            for cp in weight_copies(nxt):
                cp.start()

    @pl.when(active)
    def _():
        x = x_ref[...]
        a = jnp.minimum(jnp.dot(x, wg_s[...], preferred_element_type=F32) + bg_ref[...], SWIGLU_LIMIT)
        b = jnp.clip(jnp.dot(x, wu_s[...], preferred_element_type=F32) + bu_ref[...],
                     -SWIGLU_LIMIT, SWIGLU_LIMIT)
        act = (b + 1.0) * a * _sigmoid(SWIGLU_ALPHA * a)
        y = jnp.dot(act.astype(BF16), wd_s[...], preferred_element_type=F32) + bd_ref[...]
        o_ref[...] = y.astype(o_ref.dtype)

    @pl.when(jnp.logical_not(active))
    def _():
        o_ref[...] = jnp.zeros_like(o_ref)


def _experts_call(tile_expert, n_used, next_expert, xs, layer, wg, bg, wu, bu, wd, bd):
    rows, d = xs.shape
    depth, n_exp, _, de = wg.shape
    pick = lambda j, te, nu, nxt: (layer, te[j], 0, 0)
    hbm = pl.BlockSpec(memory_space=pl.ANY)
    grid_spec = pltpu.PrefetchScalarGridSpec(
        num_scalar_prefetch=3,
        grid=(rows // EXP_TILE,),
        in_specs=[
            pl.BlockSpec((EXP_TILE, d), lambda j, te, nu, nxt: (j, 0)),
            hbm,
            pl.BlockSpec((None, None, 1, de), pick),
            hbm,
            pl.BlockSpec((None, None, 1, de), pick),
            hbm,
            pl.BlockSpec((None, None, 1, d), pick),
        ],
        out_specs=pl.BlockSpec((EXP_TILE, d), lambda j, te, nu, nxt: (j, 0)),
        scratch_shapes=[pltpu.VMEM((d, de), F32), pltpu.VMEM((d, de), F32), pltpu.VMEM((de, d), F32),
                        pltpu.VMEM((d, de), BF16), pltpu.VMEM((d, de), BF16), pltpu.VMEM((de, d), BF16),
                        pltpu.SemaphoreType.DMA((3,))],
    )
    return pl.pallas_call(
        functools.partial(_experts_kernel, layer=layer),
        grid_spec=grid_spec,
        out_shape=jax.ShapeDtypeStruct((rows, d), BF16),
        compiler_params=_cparams(("arbitrary",)),
        name="moe_experts",
    )(tile_expert, n_used, next_expert, xs, wg, bg.reshape(depth, n_exp, 1, de), wu,
      bu.reshape(depth, n_exp, 1, de), wd, bd.reshape(depth, n_exp, 1, d))


def _combine_kernel(y_ref, p_ref, z_ref, mod_ref, fg_ref, o_ref, *, final):
    p = p_ref[...]
    acc = p[:, 0:1] * y_ref[0].astype(F32)
    for s in range(1, TOP_K):
        acc += p[:, s:s + 1] * y_ref[s].astype(F32)
    z = z_ref[...] + mod_ref[5:6, :] * acc
    if final:
        ms = jnp.mean(z * z, axis=-1, keepdims=True)
        z = z * lax.rsqrt(ms + RMS_EPS) * fg_ref[...]
    o_ref[...] = z


def _combine_call(yg, gates_t, z, n_rows, mod, fg, *, mod_row, final):
    d = z.shape[1]
    tm = 256
    return pl.pallas_call(
        functools.partial(_combine_kernel, final=final),
        grid=(n_rows // tm,),
        in_specs=[
            pl.BlockSpec((TOP_K, tm, d), lambda i: (0, i, 0)),
            pl.BlockSpec((tm, TOP_K), lambda i: (i, 0)),
            pl.BlockSpec((tm, d), lambda i: (i, 0)),
            pl.BlockSpec((None, N_MOD, d), lambda i: (mod_row(i * tm), 0, 0)),
            pl.BlockSpec((1, d), lambda i: (0, 0)),
        ],
        out_specs=pl.BlockSpec((tm, d), lambda i: (i, 0)),
        out_shape=jax.ShapeDtypeStruct((n_rows, d), F32),
        compiler_params=_cparams(("parallel",)),
        name="moe_combine",
    )(yg, gates_t, z, mod, fg.reshape(1, d))


def _invert_kernel(pos_ref, init_hbm, out_ref, *, n_tok, n_slots):
    pltpu.sync_copy(init_hbm, out_ref)
    for s in range(n_slots):
        def body(t, carry, s=s):
            out_ref[pos_ref[s * n_tok + t]] = t
            return carry

        lax.fori_loop(0, n_tok, body, 0, unroll=8)


def _invert_call(pos, n_rows):
    k, n = pos.shape
    smem = pl.BlockSpec(memory_space=pltpu.SMEM)
    return pl.pallas_call(
        functools.partial(_invert_kernel, n_tok=n, n_slots=k),
        in_specs=[smem, pl.BlockSpec(memory_space=pl.ANY)],
        out_specs=smem,
        out_shape=jax.ShapeDtypeStruct((n_rows,), jnp.int32),
        name="moe_row_tokens",
    )(pos.reshape(-1), jnp.arange(n_rows, dtype=jnp.int32) % n)


def _dispatch_plan(idx, rank, counts):
    k, n = idx.shape
    n_exp = counts.shape[0]
    padded = ((counts + EXP_TILE - 1) // EXP_TILE) * EXP_TILE
    ends = jnp.cumsum(padded)
    offsets = ends - padded
    onehot = idx[:, :, None] == jnp.arange(n_exp, dtype=jnp.int32)[None, None, :]
    pos = rank + jnp.sum(jnp.where(onehot, offsets[None, None, :], 0), axis=-1)
    rows = k * n + n_exp * EXP_TILE
    row_token = _invert_call(pos, rows)
    n_tiles = rows // EXP_TILE
    tile_start = jnp.arange(n_tiles, dtype=jnp.int32) * EXP_TILE
    n_used = (ends[-1] // EXP_TILE).astype(jnp.int32)
    last_start = jnp.maximum(n_used - 1, 0) * EXP_TILE
    tile_expert = jnp.sum((ends[None, :] <= jnp.minimum(tile_start, last_start)[:, None]).astype(jnp.int32), axis=1)
    tile_expert = jnp.minimum(tile_expert, n_exp - 1)
    e_ids = jnp.arange(n_exp, dtype=jnp.int32)
    later_used = jnp.logical_and(e_ids[None, :] > e_ids[:, None], (counts > 0)[None, :])
    next_expert = jnp.min(jnp.where(later_used, e_ids[None, :], n_exp), axis=1)
    next_expert = jnp.where(next_expert == n_exp, -1, next_expert).astype(jnp.int32)
    return pos, row_token, tile_expert, n_used.reshape(1), next_expert


def kernel(x, c, ctx, c_ctx, ada_w, ada_b, norm1_g, w_in, ret_decay_fwd, ret_decay_bwd, w_ret_o,
           conv_dw_w, conv_dw_b, conv_ln_g, conv_ln_b, w_conv_o, w_out, norm2_g, router_w, router_b,
           exp_w_gate, exp_b_gate, exp_w_up, exp_b_up, exp_w_down, exp_b_down, final_norm_g):
    batch, seq, d = x.shape
    n_ctx = ctx.shape[1]
    depth = ada_w.shape[0]
    n_exp = router_w.shape[2]
    assert batch + 1 <= MOD_ROWS and seq % CHUNK == 0 and n_ctx == CHUNK
    n_groups = 2 if batch % 2 == 0 else 1
    gb = batch // n_groups
    n_lat = gb * seq
    tm_big = _pick_tile(seq, 1024)
    while (gb * n_ctx) % tm_big:
        tm_big //= 2

    def make_mod_row(g):
        return lambda row0: jnp.where(row0 >= n_lat, batch, g * gb + row0 // seq)

    mod_rows = [make_mod_row(g) for g in range(n_groups)]
    cond = jnp.concatenate([c, c_ctx[None, :], jnp.zeros((MOD_ROWS - batch - 1, d), F32)], axis=0)
    mod_all = _ada_call(cond, ada_w, ada_b).reshape(depth, MOD_ROWS, N_MOD, d)
    rope_t = _rope_tables(seq, n_ctx)
    cw = conv_dw_w.shape[2]
    col0 = 4 * RET_W // cw
    s_col = (4 * RET_W + 2 * cw) // d

    zs = [jnp.concatenate([x[g * gb:(g + 1) * gb].reshape(n_lat, d),
                           ctx[g * gb:(g + 1) * gb].reshape(gb * n_ctx, d)], axis=0) for g in range(n_groups)]

    for l in range(depth):
        mod = mod_all[l]
        last = l == depth - 1
        w_in_l = w_in[l].astype(BF16)
        w_ret_l = w_ret_o[l].astype(BF16)
        w_conv_l = w_conv_o[l].astype(BF16)
        w_out_l = w_out[l].astype(BF16)
        rw_t = router_w[l].T.astype(BF16)
        dmask, tab, dec = _retention_tables(ret_decay_fwd[l], ret_decay_bwd[l])
        staged = []
        for g in range(n_groups):
            z, mod_row = zs[g], mod_rows[g]
            p = _inproj_call(z, norm1_g[l], mod, w_in_l, mod_row=mod_row, tm=tm_big)
            r = _retention_call(p, dec, rope_t, dmask, tab, batch=gb, seq=seq, ctx=n_ctx)
            u = _conv_call(p, conv_dw_w[l], conv_dw_b[l], conv_ln_g[l], conv_ln_b[l],
                           batch=gb, seq=seq, ctx=n_ctx, col_a=col0, col_b=col0 + 1)
            z = _mixout_call(r, u, p, z, mod, w_ret_l, w_conv_l, w_out_l, mod_row=mod_row,
                             col_sr=s_col, col_sc=s_col + 1)
            n_rows = n_lat if last else z.shape[0]
            h, idx, gates, rank, counts = _router_call(z, n_rows, norm2_g[l], mod, rw_t, router_b[l],
                                                       mod_row=mod_row, tm=tm_big)
            pos, row_token, tile_expert, n_used, next_expert = _dispatch_plan(idx, rank, counts[:, 0])
            if g == 0 and l > 0:
                row_token = _after(row_token, zs[n_groups - 1])
            xs = h.at[row_token].get(mode="promise_in_bounds")
            staged.append((z, n_rows, gates, pos, tile_expert, n_used, next_expert, xs))

        def combine(g, yg):
            z, n_rows, gates = staged[g][:3]
            return _combine_call(yg, gates.T, z, n_rows, mod, final_norm_g, mod_row=mod_rows[g], final=last)

        yg_prev = None
        for g in range(n_groups):
            z, n_rows, gates, pos, tile_expert, n_used, next_expert, xs = staged[g]
            ys = _experts_call(tile_expert, n_used, next_expert, xs, l, exp_w_gate, exp_b_gate, exp_w_up, exp_b_up,
                               exp_w_down, exp_b_down)
            if g > 0:
                zs[g - 1] = combine(g - 1, yg_prev)
                pos = _after(pos, zs[g - 1])
            yg_prev = ys.at[pos].get(mode="promise_in_bounds").reshape(TOP_K, n_rows, d)
        zs[n_groups - 1] = combine(n_groups - 1, yg_prev)
    return jnp.concatenate([z.reshape(gb, seq, d) for z in zs], axis=0)
```
